```python
import math
import jax
import jax.numpy as jnp
from jax import lax
import numpy as np

D_MODEL = 2048
BATCH = 8
SEQ = 2048
DEPTH = 1

MIX_WIDTH = D_MODEL
D_SSD = MIX_WIDTH // 2
D_ATTN = MIX_WIDTH - D_SSD
SSD_HEAD_DIM = 64
N_SSD_HEADS = D_SSD // SSD_HEAD_DIM
SSD_GROUPS = 2
D_STATE = 128
CONV_W = 5
CONV_CH = D_SSD + 2 * SSD_GROUPS * D_STATE
CHUNK = 256
ATTN_HEAD_DIM = 64
N_ATTN_HEADS = D_ATTN // (2 * ATTN_HEAD_DIM)
Q_BLOCK = 128
N_EXPERTS = 64
N_EXPERT_GROUPS = 8
TOPK_GROUPS = 4
TOP_K = 8
D_EXPERT = 512
ROUTED_SCALE = 2.5
BLOCK_ROWS = 256
LN_EPS = 1e-5
RMS_EPS = 1e-5
ALPHA = (2 * DEPTH) ** 0.25
BETA = (8 * DEPTH) ** -0.25
_S0 = D_SSD + CONV_CH + 2 * N_SSD_HEADS
PROJ_SPLITS = (D_SSD, D_SSD + CONV_CH, D_SSD + CONV_CH + N_SSD_HEADS, _S0, _S0 + D_ATTN, _S0 + 2 * D_ATTN)
PROJ_WIDTH = _S0 + 3 * D_ATTN

kernel_name = "hybrid_ssd_diffattn_moe_encoder"


def layer_norm(x, g, b):
    xf = x.astype(jnp.float32)
    mu = jnp.mean(xf, -1, keepdims=True)
    var = jnp.mean(jnp.square(xf - mu), -1, keepdims=True)
    return ((xf - mu) * lax.rsqrt(var + LN_EPS) * g + b).astype(x.dtype)


def rms_norm(x, w):
    xf = x.astype(jnp.float32)
    return xf * lax.rsqrt(jnp.mean(jnp.square(xf), -1, keepdims=True) + RMS_EPS) * w


def centred_depthwise_conv(x, w, b):
    y = lax.conv_general_dilated(x, w[:, None, :].astype(x.dtype), (1,), [(CONV_W // 2, CONV_W // 2)],
                                 dimension_numbers=('NWC', 'WIO', 'NWC'), feature_group_count=x.shape[-1])
    return y + b


def ssd_scan(x, dt, a, b_mat, c_mat):
    f32 = jnp.float32
    bsz, seqlen, nh, hp = x.shape
    ng = b_mat.shape[2]
    r = nh // ng
    nc = -(-seqlen // CHUNK)
    pad = nc * CHUNK - seqlen
    padl = lambda t: jnp.pad(t.astype(f32), [(0, 0), (0, pad)] + [(0, 0)] * (t.ndim - 2))
    x, dt, b_mat, c_mat = padl(x), padl(dt), padl(b_mat), padl(c_mat)
    xc = (x * dt[..., None]).reshape(bsz, nc, CHUNK, ng, r, hp)
    adt = (dt * a.astype(f32)).reshape(bsz, nc, CHUNK, ng, r)
    bc = b_mat.reshape(bsz, nc, CHUNK, ng, D_STATE)
    cc = c_mat.reshape(bsz, nc, CHUNK, ng, D_STATE)
    a_cs = jnp.cumsum(adt, axis=2)
    seg = a_cs[:, :, :, None] - a_cs[:, :, None, :]
    tri = jnp.tril(jnp.ones((CHUNK, CHUNK), bool))
    decay = jnp.exp(jnp.where(tri[:, :, None, None], seg, -jnp.inf))
    cb = jnp.einsum('bclgn,bcsgn->bclsg', cc, bc)
    y_diag = jnp.einsum('bclsgr,bcsgrp->bclgrp', cb[..., None] * decay, xc)
    decay_to_end = jnp.exp(a_cs[:, :, -1:] - a_cs)
    states = jnp.einsum('bcsgn,bcsgr,bcsgrp->bcgrpn', bc, decay_to_end, xc)
    chunk_decay = jnp.exp(a_cs[:, :, -1])

    def step(h, inp):
        s_c, d_c = inp
        return h * d_c[..., None, None] + s_c, h

    h0 = jnp.zeros((bsz, ng, r, hp, D_STATE), f32)
    _, prev = lax.scan(step, h0, (jnp.swapaxes(states, 0, 1), jnp.swapaxes(chunk_decay, 0, 1)))
    prev = jnp.swapaxes(prev, 0, 1)
    y_off = jnp.einsum('bclgn,bcgrpn,bclgr->bclgrp', cc, prev, jnp.exp(a_cs))
    y = (y_diag + y_off).reshape(bsz, nc * CHUNK, nh, hp)
    return y[:, :seqlen]


def alibi_slopes(n):
    return 2.0 ** (-8.0 * jnp.arange(1, n + 1, dtype=jnp.float32) / n)


def diff_attention(q, k, v, lam, norm_w, lambda_init):
    f32 = jnp.float32
    bsz, seqlen = q.shape[:2]
    nb = seqlen // Q_BLOCK
    qb = q.reshape(bsz, nb, Q_BLOCK, N_ATTN_HEADS, 2, ATTN_HEAD_DIM).transpose(1, 0, 3, 4, 2, 5)
    kt = k.transpose(0, 2, 3, 1, 4)
    vt = v.transpose(0, 2, 1, 3)
    slopes = alibi_slopes(N_ATTN_HEADS)
    kpos = jnp.arange(seqlen, dtype=f32)
    scale = ATTN_HEAD_DIM ** -0.5

    def one_block(args):
        q_blk, start = args
        qpos = start + jnp.arange(Q_BLOCK, dtype=f32)
        bias = -slopes[:, None, None] * jnp.abs(qpos[:, None] - kpos[None, :])
        s = jnp.einsum('bhcqd,bhckd->bhcqk', q_blk, kt).astype(f32) * scale + bias[None, :, None]
        p = jax.nn.softmax(s, axis=-1)
        att = p[:, :, 0] - lam * p[:, :, 1]
        return jnp.einsum('bhqk,bhke->bhqe', att.astype(vt.dtype), vt)

    starts = jnp.arange(nb, dtype=f32) * Q_BLOCK
    out = lax.map(one_block, (qb, starts))
    out = out.transpose(1, 0, 3, 2, 4).reshape(bsz, seqlen, N_ATTN_HEADS, 2 * ATTN_HEAD_DIM)
    out = rms_norm(out, norm_w) * (1.0 - lambda_init)
    return out.reshape(bsz, seqlen, D_ATTN)


def mixer(h, w_in, conv_w, conv_b, dt_bias_f, dt_bias_b, a_log_f, a_log_b, d_skip, ssd_norm_w,
          lambda_q1, lambda_k1, lambda_q2, lambda_k2, attn_norm_w, w_out, lambda_init):
    f32 = jnp.float32
    bsz, seqlen, _ = h.shape
    proj = h @ w_in
    z, xbc, dt_f, dt_b, q, k, v = jnp.split(proj, PROJ_SPLITS, axis=-1)
    xbc = jax.nn.silu(centred_depthwise_conv(xbc, conv_w, conv_b))
    xs, bm, cm = jnp.split(xbc, (D_SSD, D_SSD + SSD_GROUPS * D_STATE), axis=-1)
    xs = xs.reshape(bsz, seqlen, N_SSD_HEADS, SSD_HEAD_DIM)
    bm = bm.reshape(bsz, seqlen, SSD_GROUPS, D_STATE)
    cm = cm.reshape(bsz, seqlen, SSD_GROUPS, D_STATE)
    dtf = jax.nn.softplus(dt_f.astype(f32) + dt_bias_f)
    dtb = jax.nn.softplus(dt_b.astype(f32) + dt_bias_b)
    a_f = -jnp.exp(a_log_f.astype(f32))
    a_b = -jnp.exp(a_log_b.astype(f32))
    y_f = ssd_scan(xs, dtf, a_f, bm, cm)
    flip = lambda t: jnp.flip(t, axis=1)
    y_b = flip(ssd_scan(flip(xs), flip(dtb), a_b, flip(bm), flip(cm)))
    y = y_f + y_b + d_skip[:, None] * xs.astype(f32)
    y = y.reshape(bsz, seqlen, D_SSD) * jax.nn.silu(z.astype(f32))
    y = rms_norm(y.reshape(bsz, seqlen, SSD_GROUPS, D_SSD // SSD_GROUPS), 1.0)
    y_ssd = y.reshape(bsz, seqlen, D_SSD) * ssd_norm_w
    q = q.reshape(bsz, seqlen, N_ATTN_HEADS, 2, ATTN_HEAD_DIM)
    k = k.reshape(bsz, seqlen, N_ATTN_HEADS, 2, ATTN_HEAD_DIM)
    v = v.reshape(bsz, seqlen, N_ATTN_HEADS, 2 * ATTN_HEAD_DIM)
    lam = (jnp.exp(jnp.sum(lambda_q1.astype(f32) * lambda_k1.astype(f32)))
           - jnp.exp(jnp.sum(lambda_q2.astype(f32) * lambda_k2.astype(f32))) + lambda_init)
    y_attn = diff_attention(q, k, v, lam, attn_norm_w, lambda_init)
    u = jnp.concatenate([y_ssd.astype(h.dtype), y_attn.astype(h.dtype)], axis=-1)
    return u @ w_out


def moe(h, w_router, router_bias, w_gate_e, w_up_e, w_down_e, w_gate_s, w_up_s, w_down_s):
    f32 = jnp.float32
    bsz, seqlen, d = h.shape
    t = h.reshape(-1, d)
    n_tok = t.shape[0]
    scores = jax.nn.sigmoid((t @ w_router).astype(f32))
    biased = scores + router_bias.astype(f32)
    grp = biased.reshape(n_tok, N_EXPERT_GROUPS, N_EXPERTS // N_EXPERT_GROUPS)
    grp_score = jnp.sum(lax.top_k(grp, 2)[0], axis=-1)
    _, top_g = lax.top_k(grp_score, TOPK_GROUPS)
    gmask = jnp.any(top_g[:, :, None] == jnp.arange(N_EXPERT_GROUPS)[None, None, :], axis=1)
    emask = jnp.repeat(gmask, N_EXPERTS // N_EXPERT_GROUPS, axis=1)
    _, eidx = lax.top_k(jnp.where(emask, biased, -jnp.inf), TOP_K)
    gate = jnp.take_along_axis(scores, eidx, axis=1)
    gate = gate / jnp.sum(gate, -1, keepdims=True) * ROUTED_SCALE
    n_assign = n_tok * TOP_K
    flat_e = eidx.reshape(-1)
    flat_tok = jnp.repeat(jnp.arange(n_tok, dtype=jnp.int32), TOP_K)
    flat_g = gate.reshape(-1)
    order = jnp.argsort(flat_e)
    se, st, sg = flat_e[order], flat_tok[order], flat_g[order]
    counts = jnp.bincount(flat_e, length=N_EXPERTS)
    starts = jnp.cumsum(counts) - counts
    padded = (counts + BLOCK_ROWS - 1) // BLOCK_ROWS * BLOCK_ROWS
    pends = jnp.cumsum(padded)
    pstarts = pends - padded
    dest = pstarts[se] + (jnp.arange(n_assign) - starts[se])
    n_blocks = -(-n_assign // BLOCK_ROWS) + N_EXPERTS
    row_tok = jnp.full((n_blocks * BLOCK_ROWS,), n_tok, jnp.int32).at[dest].set(st)
    row_gate = jnp.zeros((n_blocks * BLOCK_ROWS,), f32).at[dest].set(sg)
    block_exp = jnp.minimum(jnp.searchsorted(pends, jnp.arange(n_blocks) * BLOCK_ROWS, side='right'), N_EXPERTS - 1)
    t_pad = jnp.concatenate([t, jnp.zeros((1, d), t.dtype)], axis=0)

    def body(acc, blk):
        tok, g, e = blk
        xb = t_pad[tok]
        hb = jax.nn.silu(xb @ w_gate_e[e]) * (xb @ w_up_e[e])
        yb = (hb @ w_down_e[e]).astype(f32) * g[:, None]
        return acc.at[tok].add(yb), None

    acc, _ = lax.scan(body, jnp.zeros((n_tok + 1, d), f32),
                      (row_tok.reshape(n_blocks, BLOCK_ROWS), row_gate.reshape(n_blocks, BLOCK_ROWS), block_exp))
    routed = acc[:n_tok]
    shared = ((jax.nn.silu(t @ w_gate_s) * (t @ w_up_s)) @ w_down_s).astype(f32)
    return (routed + shared).astype(h.dtype).reshape(bsz, seqlen, d)


def setup_inputs(seed: int = 0) -> dict:
    key = jax.random.key(seed)
    ks = jax.random.split(key, 32)
    f32 = jnp.float32
    L = DEPTH
    nrm = lambda k, shape, s: jax.random.normal(k, shape, f32) * s

    def dt_bias(k):
        u = jax.random.uniform(k, (L, N_SSD_HEADS), f32)
        dt = jnp.exp(u * (math.log(0.1) - math.log(0.001)) + math.log(0.001))
        return dt + jnp.log(-jnp.expm1(-dt))

    return {
        'x': nrm(ks[0], (BATCH, SEQ, D_MODEL), 1.0),
        'w_in': nrm(ks[1], (L, D_MODEL, PROJ_WIDTH), D_MODEL ** -0.5),
        'conv_w': nrm(ks[2], (L, CONV_W, CONV_CH), CONV_W ** -0.5),
        'conv_b': nrm(ks[3], (L, CONV_CH), 0.01),
        'dt_bias_f': dt_bias(ks[4]),
        'dt_bias_b': dt_bias(ks[5]),
        'a_log_f': jnp.log(jax.random.uniform(ks[6], (L, N_SSD_HEADS), f32, 1.0, 16.0)),
        'a_log_b': jnp.log(jax.random.uniform(ks[7], (L, N_SSD_HEADS), f32, 1.0, 16.0)),
        'd_skip': 1.0 + nrm(ks[8], (L, N_SSD_HEADS), 0.01),
        'ssd_norm_w': 1.0 + nrm(ks[9], (L, D_SSD), 0.01),
        'lambda_q1': nrm(ks[10], (L, ATTN_HEAD_DIM), 0.1),
        'lambda_k1': nrm(ks[11], (L, ATTN_HEAD_DIM), 0.1),
        'lambda_q2': nrm(ks[12], (L, ATTN_HEAD_DIM), 0.1),
        'lambda_k2': nrm(ks[13], (L, ATTN_HEAD_DIM), 0.1),
        'attn_norm_w': 1.0 + nrm(ks[14], (L, 2 * ATTN_HEAD_DIM), 0.01),
        'w_out': nrm(ks[15], (L, MIX_WIDTH, D_MODEL), MIX_WIDTH ** -0.5 * BETA),
        'ln1_g': 1.0 + nrm(ks[16], (L, D_MODEL), 0.01),
        'ln1_b': nrm(ks[17], (L, D_MODEL), 0.01),
        'w_router': nrm(ks[18], (L, D_MODEL, N_EXPERTS), D_MODEL ** -0.5),
        'router_bias': nrm(ks[19], (L, N_EXPERTS), 0.01),
        'w_gate_e': nrm(ks[20], (L, N_EXPERTS, D_MODEL, D_EXPERT), D_MODEL ** -0.5),
        'w_up_e': nrm(ks[21], (L, N_EXPERTS, D_MODEL, D_EXPERT), D_MODEL ** -0.5),
        'w_down_e': nrm(ks[22], (L, N_EXPERTS, D_EXPERT, D_MODEL), D_EXPERT ** -0.5 * BETA),
        'w_gate_s': nrm(ks[23], (L, D_MODEL, D_EXPERT), D_MODEL ** -0.5),
        'w_up_s': nrm(ks[24], (L, D_MODEL, D_EXPERT), D_MODEL ** -0.5),
        'w_down_s': nrm(ks[25], (L, D_EXPERT, D_MODEL), D_EXPERT ** -0.5 * BETA),
        'ln2_g': 1.0 + nrm(ks[26], (L, D_MODEL), 0.01),
        'ln2_b': nrm(ks[27], (L, D_MODEL), 0.01),
    }


def reference(x, w_in, conv_w, conv_b, dt_bias_f, dt_bias_b, a_log_f, a_log_b, d_skip, ssd_norm_w,
              lambda_q1, lambda_k1, lambda_q2, lambda_k2, attn_norm_w, w_out, ln1_g, ln1_b,
              w_router, router_bias, w_gate_e, w_up_e, w_down_e, w_gate_s, w_up_s, w_down_s,
              ln2_g, ln2_b):
    h = x
    for l in range(DEPTH):
        lambda_init = 0.8 - 0.6 * math.exp(-0.3 * l)
        mix = mixer(h, w_in[l], conv_w[l], conv_b[l], dt_bias_f[l], dt_bias_b[l], a_log_f[l], a_log_b[l],
                    d_skip[l], ssd_norm_w[l], lambda_q1[l], lambda_k1[l], lambda_q2[l], lambda_k2[l],
                    attn_norm_w[l], w_out[l], lambda_init)
        h = layer_norm(ALPHA * h + mix, ln1_g[l], ln1_b[l])
        ffn = moe(h, w_router[l], router_bias[l], w_gate_e[l], w_up_e[l], w_down_e[l],
                  w_gate_s[l], w_up_s[l], w_down_s[l])
        h = layer_norm(ALPHA * h + ffn, ln2_g[l], ln2_b[l])
    return h
```

```python
import functools
import math

import jax
import jax.numpy as jnp
from jax import lax
from jax.experimental import pallas as pl
from jax.experimental.pallas import tpu as pltpu

f32 = jnp.float32
bf16 = jnp.bfloat16
i32 = jnp.int32

D_MODEL = 2048
D_SSD = 1024
D_ATTN = 1024
SSD_HEAD_DIM = 64
N_SSD_HEADS = 16
SSD_GROUPS = 2
HEADS_PER_GROUP = N_SSD_HEADS // SSD_GROUPS
D_STATE = 128
CONV_W = 5
CONV_CH = D_SSD + 2 * SSD_GROUPS * D_STATE
CHUNK = 256
ATTN_HEAD_DIM = 64
N_ATTN_HEADS = 8
N_EXPERTS = 64
N_EXPERT_GROUPS = 8
EXPERTS_PER_GROUP = N_EXPERTS // N_EXPERT_GROUPS
TOPK_GROUPS = 4
TOP_K = 8
D_EXPERT = 512
ROUTED_SCALE = 2.5
BLOCK_ROWS = 256
LN_EPS = 1e-5
RMS_EPS = 1e-5
DEPTH = 1
ALPHA = (2 * DEPTH) ** 0.25

LANES = 128
BF16_SUBLANES = 16
VMEM_LIMIT = 56 * 1024 * 1024

PROJ_MAIN = D_SSD + CONV_CH + 3 * D_ATTN
COL_XBC = D_SSD
COL_Q = D_SSD + CONV_CH
COL_K = COL_Q + D_ATTN
COL_V = COL_K + D_ATTN

HIGHEST = lax.Precision.HIGHEST
HI_MASK = -65536


def _sigmoid(x):
    return 1.0 / (1.0 + jnp.exp(-x))


def _silu(x):
    return x * _sigmoid(x)


def _softplus(x):
    return jnp.maximum(x, 0.0) + jnp.log(1.0 + jnp.exp(-jnp.abs(x)))


def _pack_bf16_pairs(v):
    n = v.shape[1] // 2
    u = pltpu.bitcast(v.astype(bf16).astype(f32), i32)
    return lax.shift_right_logical(u[:, :n], 16) | (u[:, n:] & HI_MASK)


def _unpack_bf16_pairs(p):
    lo = pltpu.bitcast(lax.shift_left(p, 16), f32)
    hi = pltpu.bitcast(p & HI_MASK, f32)
    return jnp.concatenate([lo, hi], axis=1)


def _layer_norm(r, g, b):
    mu = jnp.mean(r, axis=-1, keepdims=True)
    c = r - mu
    var = jnp.mean(c * c, axis=-1, keepdims=True)
    return c * lax.rsqrt(var + LN_EPS) * g + b


def _params(*sem):
    return pltpu.CompilerParams(dimension_semantics=sem, vmem_limit_bytes=VMEM_LIMIT)


def _inproj_kernel(x_ref, w_ref, wdt_ref, o_ref, dt_ref, xb_ref):
    @pl.when(pl.program_id(1) == 0)
    def _():
        xb_ref[...] = x_ref[...].astype(bf16)
        dt_ref[...] = jnp.dot(xb_ref[...], wdt_ref[...], preferred_element_type=f32)

    o_ref[...] = jnp.dot(xb_ref[...], w_ref[...], preferred_element_type=f32).astype(bf16)


def _in_projection(x2, w_main, w_dt):
    n_tok = x2.shape[0]
    tm = min(1024, n_tok)
    tn = 512
    return pl.pallas_call(
        _inproj_kernel,
        grid=(n_tok // tm, PROJ_MAIN // tn),
        in_specs=[
            pl.BlockSpec((tm, D_MODEL), lambda i, j: (i, 0)),
            pl.BlockSpec((D_MODEL, tn), lambda i, j: (0, j)),
            pl.BlockSpec((D_MODEL, LANES), lambda i, j: (0, 0)),
        ],
        out_specs=[
            pl.BlockSpec((tm, tn), lambda i, j: (i, j)),
            pl.BlockSpec((tm, LANES), lambda i, j: (i, 0)),
        ],
        out_shape=[
            jax.ShapeDtypeStruct((n_tok, PROJ_MAIN), bf16),
            jax.ShapeDtypeStruct((n_tok, LANES), f32),
        ],
        scratch_shapes=[pltpu.VMEM((tm, D_MODEL), bf16)],
        compiler_params=_params("parallel", "arbitrary"),
        name="in_projection",
    )(x2, w_main, w_dt)


CONV_HALO = 16


def _conv_kernel(x_ref, w_ref, b_ref, o_ref, pad_ref):
    seq = x_ref.shape[1]
    tc = x_ref.shape[2]
    zeros = jnp.zeros((CONV_HALO, tc), f32)
    pad_ref[0:CONV_HALO, :] = zeros
    pad_ref[seq + CONV_HALO:seq + 2 * CONV_HALO, :] = zeros
    w = w_ref[...]
    b = b_ref[...]
    win = CHUNK + 2 * CONV_HALO

    def fill(i, carry):
        base = pl.multiple_of(i * CHUNK, CHUNK)
        pad_ref[pl.ds(base + CONV_HALO, CHUNK), :] = x_ref[0, pl.ds(base, CHUNK), :].astype(f32)
        return carry

    lax.fori_loop(0, seq // CHUNK, fill, 0)

    def body(i, carry):
        base = pl.multiple_of(i * CHUNK, CHUNK)
        xe = pad_ref[pl.ds(base, win), :]
        acc = jnp.zeros((CHUNK, tc), f32) + b
        for k in range(CONV_W):
            d = k - CONV_W // 2
            r = xe if d == 0 else pltpu.roll(xe, (-d) % win, 0)
            acc = acc + r[CONV_HALO:CONV_HALO + CHUNK, :] * w[k:k + 1, :]
        o_ref[0, pl.ds(base, CHUNK), :] = _silu(acc).astype(bf16)
        return carry

    lax.fori_loop(0, seq // CHUNK, body, 0)


def _conv_silu(proj3, conv_w8, conv_b2):
    bsz, seq, _ = proj3.shape
    tc = 256
    col0 = COL_XBC // tc
    return pl.pallas_call(
        _conv_kernel,
        grid=(bsz, CONV_CH // tc),
        in_specs=[
            pl.BlockSpec((1, seq, tc), lambda b, j: (b, 0, col0 + j)),
            pl.BlockSpec((8, tc), lambda b, j: (0, j)),
            pl.BlockSpec((1, tc), lambda b, j: (0, j)),
        ],
        out_specs=pl.BlockSpec((1, seq, tc), lambda b, j: (b, 0, j)),
        out_shape=jax.ShapeDtypeStruct((bsz, seq, CONV_CH), bf16),
        scratch_shapes=[pltpu.VMEM((seq + 2 * CONV_HALO, tc), f32)],
        compiler_params=_params("parallel", "parallel"),
        name="conv_silu",
    )(proj3, conv_w8, conv_b2)


def _ssd_cumsums(dt_ref, bias_ref, arow_ref):
    dtv = _softplus(dt_ref[0] + bias_ref[...])
    adt = dtv * arow_ref[...]
    li = lax.broadcasted_iota(i32, (CHUNK, CHUNK), 0)
    si = lax.broadcasted_iota(i32, (CHUNK, CHUNK), 1)
    tril = (si <= li).astype(f32)
    triu = (si >= li).astype(f32)
    lane = lax.broadcasted_iota(i32, (CHUNK, LANES), 1)
    pre = jnp.dot(tril, adt, precision=HIGHEST, preferred_element_type=f32)
    suf = jnp.dot(triu, adt, precision=HIGHEST, preferred_element_type=f32)
    cs = jnp.where(lane < N_SSD_HEADS, pre, suf)
    return dtv, cs


def _expand_heads(v, ex_ref):
    return jnp.dot(v, ex_ref[...], precision=HIGHEST, preferred_element_type=f32)


def _state_update(h_ref, bc, xw, decay_row, b_col0):
    gw = HEADS_PER_GROUP * SSD_HEAD_DIM
    new = []
    for g in range(SSD_GROUPS):
        bg = bc[:, b_col0 + g * D_STATE:b_col0 + (g + 1) * D_STATE]
        bgt = bg.astype(f32).T.astype(bf16)
        new.append(jnp.dot(bgt, xw[:, g * gw:(g + 1) * gw], preferred_element_type=f32))
    h_ref[...] = h_ref[...] * decay_row + jnp.concatenate(new, axis=1)


def _state_readout(h_ref, bc, c_col0):
    gw = HEADS_PER_GROUP * SSD_HEAD_DIM
    outs = []
    for g in range(SSD_GROUPS):
        cg = bc[:, c_col0 + g * D_STATE:c_col0 + (g + 1) * D_STATE]
        outs.append(jnp.dot(cg, h_ref[:, g * gw:(g + 1) * gw].astype(bf16), preferred_element_type=f32))
    return jnp.concatenate(outs, axis=1)


C_COL0 = SSD_GROUPS * D_STATE


def _ssd_fwd_kernel(x_ref, bc_ref, dt_ref, bias_ref, arow_ref, dskip_ref, exf_ref, y_ref, h_ref):
    @pl.when(pl.program_id(1) == 0)
    def _():
        h_ref[...] = jnp.zeros_like(h_ref)

    x = x_ref[0]
    bc = bc_ref[0]
    dtv, cs = _ssd_cumsums(dt_ref, bias_ref, arow_ref)
    cst = cs.T
    dtt = dtv.T
    li = lax.broadcasted_iota(i32, (CHUNK, CHUNK), 0)
    si = lax.broadcasted_iota(i32, (CHUNK, CHUNK), 1)
    lower = li > si
    upper = li < si
    lower_eq = li >= si
    lane = lax.broadcasted_iota(i32, (CHUNK, LANES), 1)
    first_half = lane < SSD_HEAD_DIM

    cb = []
    for g in range(SSD_GROUPS):
        bg = bc[:, g * D_STATE:(g + 1) * D_STATE]
        cg = bc[:, C_COL0 + g * D_STATE:C_COL0 + (g + 1) * D_STATE]
        cb.append(lax.dot_general(cg, bg, (((1,), (1,)), ((), ())), preferred_element_type=f32))

    pairs = []
    for j in range(N_SSD_HEADS // 2):
        xpair = x[:, j * LANES:(j + 1) * LANES]
        acc = None
        for h, xm in ((2 * j, jnp.where(first_half, xpair, jnp.zeros_like(xpair))),
                      (2 * j + 1, jnp.where(first_half, jnp.zeros_like(xpair), xpair))):
            hb = N_SSD_HEADS + h
            arg = jnp.where(lower_eq, cs[:, h:h + 1] - cst[h:h + 1, :], cs[:, hb:hb + 1] - cst[hb:hb + 1, :])
            dtf = dtt[h:h + 1, :]
            dtb = dtt[hb:hb + 1, :]
            wgt = jnp.where(lower, dtf, jnp.where(upper, dtb, dtf + dtb))
            m = (cb[h // HEADS_PER_GROUP] * jnp.exp(arg) * wgt).astype(bf16)
            t = jnp.dot(m, xm, preferred_element_type=f32)
            acc = t if acc is None else acc + t
        pairs.append(acc)
    y = jnp.concatenate(pairs, axis=1)

    xf = x.astype(f32)
    e_f = _expand_heads(jnp.exp(cs), exf_ref)
    y = y + _state_readout(h_ref, bc, C_COL0) * e_f + dskip_ref[...] * xf
    y_ref[0] = y

    to_end = jnp.where(lane < N_SSD_HEADS, cs[CHUNK - 1:CHUNK, :] - cs, 0.0)
    w_f = _expand_heads(jnp.exp(to_end) * dtv, exf_ref)
    _state_update(h_ref, bc, (xf * w_f).astype(bf16), e_f[CHUNK - 1:CHUNK, :], 0)


def _ssd_bwd_kernel(x_ref, bc_ref, dt_ref, z_ref, yp_ref, bias_ref, arow_ref, exb_ref, nw_ref, o_ref, h_ref):
    @pl.when(pl.program_id(1) == 0)
    def _():
        h_ref[...] = jnp.zeros_like(h_ref)

    x = x_ref[0]
    bc = bc_ref[0]
    dtv, cs = _ssd_cumsums(dt_ref, bias_ref, arow_ref)
    e_b = _expand_heads(jnp.exp(cs), exb_ref)
    y = yp_ref[0] + _state_readout(h_ref, bc, C_COL0) * e_b

    lane = lax.broadcasted_iota(i32, (CHUNK, LANES), 1)
    to_start = jnp.where((lane >= N_SSD_HEADS) & (lane < 2 * N_SSD_HEADS), cs[0:1, :] - cs, 0.0)
    w_b = _expand_heads(jnp.exp(to_start) * dtv, exb_ref)
    _state_update(h_ref, bc, (x.astype(f32) * w_b).astype(bf16), e_b[0:1, :], 0)

    y = y * _silu(z_ref[0].astype(f32))
    gw = D_SSD // SSD_GROUPS
    outs = []
    for g in range(SSD_GROUPS):
        yg = y[:, g * gw:(g + 1) * gw]
        outs.append(yg * lax.rsqrt(jnp.mean(yg * yg, axis=-1, keepdims=True) + RMS_EPS))
    o_ref[0] = (jnp.concatenate(outs, axis=1) * nw_ref[...]).astype(bf16)


def _ssd(proj3, xconv, dt3, bias_row, a_row, dskip_row, ex_f, ex_b, norm_w_row):
    bsz, seq, _ = proj3.shape
    nc = seq // CHUNK
    bc_blk = D_SSD // (2 * SSD_GROUPS * D_STATE)
    row = lambda n: pl.BlockSpec((1, n), lambda b, c: (0, 0))
    ex_spec = pl.BlockSpec((LANES, D_SSD), lambda b, c: (0, 0))
    fwd = lambda b, c: (b, c, 0)
    y_part = pl.pallas_call(
        _ssd_fwd_kernel,
        grid=(bsz, nc),
        in_specs=[
            pl.BlockSpec((1, CHUNK, D_SSD), fwd),
            pl.BlockSpec((1, CHUNK, 2 * SSD_GROUPS * D_STATE), lambda b, c: (b, c, bc_blk)),
            pl.BlockSpec((1, CHUNK, LANES), fwd),
            row(LANES), row(LANES), row(D_SSD), ex_spec,
        ],
        out_specs=pl.BlockSpec((1, CHUNK, D_SSD), fwd),
        out_shape=jax.ShapeDtypeStruct((bsz, seq, D_SSD), f32),
        scratch_shapes=[pltpu.VMEM((D_STATE, D_SSD), f32)],
        compiler_params=_params("parallel", "arbitrary"),
        name="ssd_forward_sweep",
    )(xconv, xconv, dt3, bias_row, a_row, dskip_row, ex_f)

    rev = lambda b, c: (b, nc - 1 - c, 0)
    return pl.pallas_call(
        _ssd_bwd_kernel,
        grid=(bsz, nc),
        in_specs=[
            pl.BlockSpec((1, CHUNK, D_SSD), rev),
            pl.BlockSpec((1, CHUNK, 2 * SSD_GROUPS * D_STATE), lambda b, c: (b, nc - 1 - c, bc_blk)),
            pl.BlockSpec((1, CHUNK, LANES), rev),
            pl.BlockSpec((1, CHUNK, D_SSD), rev),
            pl.BlockSpec((1, CHUNK, D_SSD), rev),
            row(LANES), row(LANES), ex_spec, row(D_SSD),
        ],
        out_specs=pl.BlockSpec((1, CHUNK, D_SSD), rev),
        out_shape=jax.ShapeDtypeStruct((bsz, seq, D_SSD), bf16),
        scratch_shapes=[pltpu.VMEM((D_STATE, D_SSD), f32)],
        compiler_params=_params("parallel", "arbitrary"),
        name="ssd_backward_sweep",
    )(xconv, xconv, dt3, proj3, y_part, bias_row, a_row, ex_b, norm_w_row)


ATTN_TQ = 256


def _attn_kernel(q_ref, k_ref, v_ref, slope_ref, lq1_ref, lk1_ref, lq2_ref, lk2_ref, nw_ref, o_ref, *, lambda_init):
    q = q_ref[0]
    k = k_ref[0]
    v = v_ref[0]
    tq = q.shape[0]
    seq = k.shape[0]
    lam = (jnp.exp(jnp.sum(lq1_ref[...] * lk1_ref[...], axis=-1, keepdims=True))
           - jnp.exp(jnp.sum(lq2_ref[...] * lk2_ref[...], axis=-1, keepdims=True)) + lambda_init)
    slope = slope_ref[0][:, 0:1]
    qpos = (pl.program_id(2) * tq + lax.broadcasted_iota(i32, (tq, seq), 0)).astype(f32)
    kpos = lax.broadcasted_iota(i32, (tq, seq), 1).astype(f32)
    bias = -slope * jnp.abs(qpos - kpos)
    lane = lax.broadcasted_iota(i32, q.shape, 1)
    first_half = lane < ATTN_HEAD_DIM
    zero = jnp.zeros_like(q)
    scale = ATTN_HEAD_DIM ** -0.5

    def softmax_map(qm):
        s = lax.dot_general(qm, k, (((1,), (1,)), ((), ())), preferred_element_type=f32) * scale + bias
        p = jnp.exp(s - jnp.max(s, axis=-1, keepdims=True))
        return p / jnp.sum(p, axis=-1, keepdims=True)

    att = softmax_map(jnp.where(first_half, q, zero)) - lam * softmax_map(jnp.where(first_half, zero, q))
    o = jnp.dot(att.astype(bf16), v, preferred_element_type=f32)
    o = o * lax.rsqrt(jnp.mean(o * o, axis=-1, keepdims=True) + RMS_EPS) * nw_ref[...] * (1.0 - lambda_init)
    o_ref[0] = o.astype(bf16)


def _diff_attention(proj3, slopes3, lq1, lk1, lq2, lk2, norm_w_row, lambda_init):
    bsz, seq, _ = proj3.shape
    tq = min(ATTN_TQ, seq)
    hd = 2 * ATTN_HEAD_DIM
    qb, kb, vb = COL_Q // hd, COL_K // hd, COL_V // hd
    vec = lambda n: pl.BlockSpec((1, n), lambda b, h, i: (0, 0))
    return pl.pallas_call(
        functools.partial(_attn_kernel, lambda_init=lambda_init),
        grid=(bsz, N_ATTN_HEADS, seq // tq),
        in_specs=[
            pl.BlockSpec((1, tq, hd), lambda b, h, i: (b, i, qb + h)),
            pl.BlockSpec((1, seq, hd), lambda b, h, i: (b, 0, kb + h)),
            pl.BlockSpec((1, seq, hd), lambda b, h, i: (b, 0, vb + h)),
            pl.BlockSpec((1, 1, LANES), lambda b, h, i: (h, 0, 0)),
            vec(ATTN_HEAD_DIM), vec(ATTN_HEAD_DIM), vec(ATTN_HEAD_DIM), vec(ATTN_HEAD_DIM), vec(hd),
        ],
        out_specs=pl.BlockSpec((1, tq, hd), lambda b, h, i: (b, i, h)),
        out_shape=jax.ShapeDtypeStruct((bsz, seq, D_ATTN), bf16),
        compiler_params=_params("parallel", "parallel", "arbitrary"),
        name="diff_attention",
    )(proj3, proj3, proj3, slopes3, lq1, lk1, lq2, lk2, norm_w_row)


def _outproj_kernel(ys_ref, ya_ref, x_ref, wa_ref, wb_ref, g_ref, b_ref, wrt_ref, h_ref, hp_ref, lg_ref):
    mix = (jnp.dot(ys_ref[...], wa_ref[...], preferred_element_type=f32)
           + jnp.dot(ya_ref[...], wb_ref[...], preferred_element_type=f32))
    h = _layer_norm(ALPHA * x_ref[...] + mix, g_ref[...], b_ref[...])
    h_ref[...] = h
    hp_ref[...] = _pack_bf16_pairs(h)
    lg_ref[...] = lax.dot_general(wrt_ref[...], h, (((1,), (1,)), ((), ())), precision=HIGHEST,
                                  preferred_element_type=f32)


def _out_projection(y_ssd2, y_attn2, x2, w_a, w_b, g_row, b_row, w_router_t):
    n_tok = x2.shape[0]
    tm = 256
    const = lambda shape: pl.BlockSpec(shape, lambda i: (0, 0))
    return pl.pallas_call(
        _outproj_kernel,
        grid=(n_tok // tm,),
        in_specs=[
            pl.BlockSpec((tm, D_SSD), lambda i: (i, 0)),
            pl.BlockSpec((tm, D_ATTN), lambda i: (i, 0)),
            pl.BlockSpec((tm, D_MODEL), lambda i: (i, 0)),
            const((D_SSD, D_MODEL)), const((D_ATTN, D_MODEL)),
            const((1, D_MODEL)), const((1, D_MODEL)), const((N_EXPERTS, D_MODEL)),
        ],
        out_specs=[
            pl.BlockSpec((tm, D_MODEL), lambda i: (i, 0)),
            pl.BlockSpec((tm, D_MODEL // 2), lambda i: (i, 0)),
            pl.BlockSpec((N_EXPERTS, tm), lambda i: (0, i)),
        ],
        out_shape=[
            jax.ShapeDtypeStruct((n_tok, D_MODEL), f32),
            jax.ShapeDtypeStruct((n_tok, D_MODEL // 2), i32),
            jax.ShapeDtypeStruct((N_EXPERTS, n_tok), f32),
        ],
        compiler_params=_params("parallel"),
        name="out_projection_ln",
    )(y_ssd2, y_attn2, x2, w_a, w_b, g_row, b_row, w_router_t)


ROUTE_TT = 512


def _route_kernel(lg_ref, bias_ref, eidx_ref, gate_ref, rel_ref, cnt_ref, run_ref):
    @pl.when(pl.program_id(0) == 0)
    def _():
        run_ref[...] = jnp.zeros_like(run_ref)

    tt = lg_ref.shape[2]
    shape3 = (N_EXPERT_GROUPS, EXPERTS_PER_GROUP, tt)
    neg = -jnp.inf
    sc = _sigmoid(lg_ref[...])
    bi = sc + bias_ref[...]
    gidx = lax.broadcasted_iota(i32, shape3, 0).astype(f32)
    jidx = lax.broadcasted_iota(i32, shape3, 1).astype(f32)
    eid = gidx * EXPERTS_PER_GROUP + jidx

    m1 = jnp.max(bi, axis=1, keepdims=True)
    i1 = jnp.min(jnp.where(bi == m1, jidx, float(EXPERTS_PER_GROUP)), axis=1, keepdims=True)
    m2 = jnp.max(jnp.where(jidx == i1, neg, bi), axis=1, keepdims=True)
    gs = m1 + m2
    gcol = lax.broadcasted_iota(i32, gs.shape, 0).astype(f32)
    gmask = jnp.zeros(gs.shape, f32)
    for _ in range(TOPK_GROUPS):
        m = jnp.max(gs, axis=0, keepdims=True)
        ig = jnp.min(jnp.where(gs == m, gcol, float(N_EXPERT_GROUPS)), axis=0, keepdims=True)
        pick = gcol == ig
        gmask = jnp.where(pick, 1.0, gmask)
        gs = jnp.where(pick, neg, gs)

    val = jnp.where(gmask > 0.0, bi, neg)
    sel = jnp.zeros(shape3, f32)
    gates = []
    for k in range(TOP_K):
        m = jnp.max(jnp.max(val, axis=1, keepdims=True), axis=0, keepdims=True)
        ie = jnp.min(jnp.min(jnp.where(val == m, eid, float(N_EXPERTS)), axis=1, keepdims=True), axis=0, keepdims=True)
        pick = eid == ie
        gates.append(jnp.sum(jnp.sum(jnp.where(pick, sc, 0.0), axis=1, keepdims=True), axis=0, keepdims=True)[0])
        eidx_ref[k:k + 1, :] = ie[0].astype(i32)
        sel = jnp.where(pick, 1.0, sel)
        val = jnp.where(pick, neg, val)
    gsum = gates[0]
    for k in range(1, TOP_K):
        gsum = gsum + gates[k]
    for k in range(TOP_K):
        gate_ref[k:k + 1, :] = gates[k] / gsum * ROUTED_SCALE

    sel2 = sel.reshape(N_EXPERTS, tt)
    s_i = lax.broadcasted_iota(i32, (tt, tt), 0)
    t_i = lax.broadcasted_iota(i32, (tt, tt), 1)
    before = (s_i < t_i).astype(bf16)
    pos = jnp.dot(sel2.astype(bf16), before, preferred_element_type=f32) + run_ref[...]
    erow = lax.broadcasted_iota(i32, (N_EXPERTS, tt), 0)
    for k in range(TOP_K):
        pick2 = erow == eidx_ref[k:k + 1, :]
        rel_ref[k:k + 1, :] = jnp.sum(jnp.where(pick2, pos, 0.0), axis=0, keepdims=True).astype(i32)
    run_ref[...] = run_ref[...] + jnp.sum(sel2, axis=1, keepdims=True)
    cnt_ref[...] = jnp.broadcast_to(run_ref[...], cnt_ref.shape)


def _route(logits3, bias3):
    n_tok = logits3.shape[2]
    tt = min(ROUTE_TT, n_tok)
    tok_blk = pl.BlockSpec((TOP_K, tt), lambda i: (0, i))
    return pl.pallas_call(
        _route_kernel,
        grid=(n_tok // tt,),
        in_specs=[
            pl.BlockSpec((N_EXPERT_GROUPS, EXPERTS_PER_GROUP, tt), lambda i: (0, 0, i)),
            pl.BlockSpec((N_EXPERT_GROUPS, EXPERTS_PER_GROUP, 1), lambda i: (0, 0, 0)),
        ],
        out_specs=[tok_blk, tok_blk, tok_blk, pl.BlockSpec((N_EXPERTS, LANES), lambda i: (0, 0))],
        out_shape=[
            jax.ShapeDtypeStruct((TOP_K, n_tok), i32),
            jax.ShapeDtypeStruct((TOP_K, n_tok), f32),
            jax.ShapeDtypeStruct((TOP_K, n_tok), i32),
            jax.ShapeDtypeStruct((N_EXPERTS, LANES), f32),
        ],
        scratch_shapes=[pltpu.VMEM((N_EXPERTS, 1), f32)],
        compiler_params=_params("arbitrary"),
        name="route_topk",
    )(logits3, bias3)


def _dest_kernel(eidx_ref, rel_ref, pstart_ref, dest_ref):
    tt = eidx_ref.shape[1]
    erow = lax.broadcasted_iota(i32, (N_EXPERTS, tt), 0)
    pstart = pstart_ref[...]
    for k in range(TOP_K):
        pick = erow == eidx_ref[k:k + 1, :]
        base = jnp.sum(jnp.where(pick, pstart, 0.0), axis=0, keepdims=True)
        dest_ref[k:k + 1, :] = rel_ref[k:k + 1, :] + base.astype(i32)


def _dest_rows(eidx, rel, pstart_col):
    n_tok = eidx.shape[1]
    tt = min(ROUTE_TT, n_tok)
    tok_blk = pl.BlockSpec((TOP_K, tt), lambda i: (0, i))
    return pl.pallas_call(
        _dest_kernel,
        grid=(n_tok // tt,),
        in_specs=[tok_blk, tok_blk, pl.BlockSpec((N_EXPERTS, 1), lambda i: (0, 0))],
        out_specs=tok_blk,
        out_shape=jax.ShapeDtypeStruct((TOP_K, n_tok), i32),
        compiler_params=_params("parallel"),
        name="dest_rows",
    )(eidx, rel, pstart_col)


MOE_TM = 256


def _dispatch_kernel(dest_ref, h_ref, xs_ref, sem):
    tm = h_ref.shape[0]

    def issue(t, carry):
        for k in range(TOP_K):
            pltpu.make_async_copy(h_ref.at[pl.ds(t, 1)], xs_ref.at[pl.ds(dest_ref[k, t], 1)], sem).start()
        return carry

    lax.fori_loop(0, tm, issue, 0)
    for _ in range(TOP_K):
        pltpu.make_async_copy(h_ref, xs_ref.at[pl.ds(0, tm)], sem).wait()


def _dispatch(dest, h_packed, n_rows):
    n_tok, width = h_packed.shape
    tm = min(MOE_TM, n_tok)
    return pl.pallas_call(
        _dispatch_kernel,
        grid=(n_tok // tm,),
        in_specs=[
            pl.BlockSpec((TOP_K, tm), lambda i: (0, i), memory_space=pltpu.SMEM),
            pl.BlockSpec((tm, width), lambda i: (i, 0)),
        ],
        out_specs=pl.BlockSpec(memory_space=pl.ANY),
        out_shape=jax.ShapeDtypeStruct((n_rows, width), i32),
        scratch_shapes=[pltpu.SemaphoreType.DMA(())],
        compiler_params=_params("arbitrary"),
        name="moe_dispatch",
    )(dest, h_packed)


def _expert_kernel(bexp_ref, nused_ref, x_ref, wg_ref, wu_ref, wd_ref, o_ref, wg_s, wu_s, wd_s):
    i = pl.program_id(0)
    changed = (i == 0) | (bexp_ref[i] != bexp_ref[jnp.maximum(i - 1, 0)])

    @pl.when(changed)
    def _():
        wg_s[...] = wg_ref[0].astype(bf16)
        wu_s[...] = wu_ref[0].astype(bf16)
        wd_s[...] = wd_ref[0].astype(bf16)

    @pl.when(i < nused_ref[0])
    def _():
        xb = _unpack_bf16_pairs(x_ref[...]).astype(bf16)
        hg = jnp.dot(xb, wg_s[...], preferred_element_type=f32)
        hu = jnp.dot(xb, wu_s[...], preferred_element_type=f32)
        hb = (_silu(hg) * hu).astype(bf16)
        o_ref[...] = _pack_bf16_pairs(jnp.dot(hb, wd_s[...], preferred_element_type=f32))


def _expert_ffn(block_exp, n_used, x_sorted, w_gate_e, w_up_e, w_down_e):
    n_rows, width = x_sorted.shape
    n_blocks = n_rows // BLOCK_ROWS
    rows = lambda i, be, nu: (jnp.minimum(i, nu[0] - 1), 0)
    wsel = lambda i, be, nu: (be[i], 0, 0)
    return pl.pallas_call(
        _expert_kernel,
        grid_spec=pltpu.PrefetchScalarGridSpec(
            num_scalar_prefetch=2,
            grid=(n_blocks,),
            in_specs=[
                pl.BlockSpec((BLOCK_ROWS, width), rows),
                pl.BlockSpec((1, D_MODEL, D_EXPERT), wsel),
                pl.BlockSpec((1, D_MODEL, D_EXPERT), wsel),
                pl.BlockSpec((1, D_EXPERT, D_MODEL), wsel),
            ],
            out_specs=pl.BlockSpec((BLOCK_ROWS, width), rows),
            scratch_shapes=[
                pltpu.VMEM((D_MODEL, D_EXPERT), bf16),
                pltpu.VMEM((D_MODEL, D_EXPERT), bf16),
                pltpu.VMEM((D_EXPERT, D_MODEL), bf16),
            ],
        ),
        out_shape=jax.ShapeDtypeStruct((n_rows, width), i32),
        compiler_params=_params("arbitrary"),
        name="moe_expert_ffn",
    )(block_exp, n_used, x_sorted, w_gate_e, w_up_e, w_down_e)


def _combine_kernel(dest_ref, gate_ref, h_ref, wg_ref, wu_ref, wd_ref, g_ref, b_ref, ys_ref, o_ref, buf, sem):
    tm = h_ref.shape[0]

    def issue(t, carry):
        for k in range(TOP_K):
            pltpu.make_async_copy(ys_ref.at[pl.ds(dest_ref[k, t], 1)], buf.at[k, pl.ds(t, 1)], sem).start()
        return carry

    lax.fori_loop(0, tm, issue, 0)

    h = h_ref[...]
    hb = h.astype(bf16)
    hg = jnp.dot(hb, wg_ref[...], preferred_element_type=f32)
    hu = jnp.dot(hb, wu_ref[...], preferred_element_type=f32)
    ffn = jnp.dot((_silu(hg) * hu).astype(bf16), wd_ref[...], preferred_element_type=f32)

    for k in range(TOP_K):
        pltpu.make_async_copy(ys_ref.at[pl.ds(0, tm)], buf.at[k], sem).wait()
    gate = gate_ref[...]
    for k in range(TOP_K):
        ffn = ffn + gate[:, k:k + 1] * _unpack_bf16_pairs(buf[k])
    o_ref[...] = _layer_norm(ALPHA * h + ffn, g_ref[...], b_ref[...])


def _combine(dest, gate_t, h1, w_gate_s, w_up_s, w_down_s, g_row, b_row, y_sorted):
    n_tok = h1.shape[0]
    width = y_sorted.shape[1]
    tm = min(MOE_TM, n_tok)
    const = lambda shape: pl.BlockSpec(shape, lambda i: (0, 0))
    return pl.pallas_call(
        _combine_kernel,
        grid=(n_tok // tm,),
        in_specs=[
            pl.BlockSpec((TOP_K, tm), lambda i: (0, i), memory_space=pltpu.SMEM),
            pl.BlockSpec((tm, TOP_K), lambda i: (i, 0)),
            pl.BlockSpec((tm, D_MODEL), lambda i: (i, 0)),
            const((D_MODEL, D_EXPERT)), const((D_MODEL, D_EXPERT)), const((D_EXPERT, D_MODEL)),
            const((1, D_MODEL)), const((1, D_MODEL)),
            pl.BlockSpec(memory_space=pl.ANY),
        ],
        out_specs=pl.BlockSpec((tm, D_MODEL), lambda i: (i, 0)),
        out_shape=jax.ShapeDtypeStruct((n_tok, D_MODEL), f32),
        scratch_shapes=[pltpu.VMEM((TOP_K, tm, width), i32), pltpu.SemaphoreType.DMA(())],
        compiler_params=_params("arbitrary"),
        name="moe_combine_ln",
    )(dest, gate_t, h1, w_gate_s, w_up_s, w_down_s, g_row, b_row, y_sorted)


def _head_expansion(first_lane):
    r = jnp.arange(LANES)[:, None]
    c = jnp.arange(D_SSD)[None, :] // SSD_HEAD_DIM
    return (r == c + first_lane).astype(f32)


def _pad_lanes(v):
    return jnp.pad(v.astype(f32), (0, LANES - v.shape[0]))[None, :]


def _layer(h3, w_in, conv_w, conv_b, dt_bias_f, dt_bias_b, a_log_f, a_log_b, d_skip, ssd_norm_w,
           lambda_q1, lambda_k1, lambda_q2, lambda_k2, attn_norm_w, w_out, ln1_g, ln1_b,
           w_router, router_bias, w_gate_e, w_up_e, w_down_e, w_gate_s, w_up_s, w_down_s,
           ln2_g, ln2_b, lambda_init):
    bsz, seq, d = h3.shape
    n_tok = bsz * seq
    x2 = h3.reshape(n_tok, d)

    n_dt = 2 * N_SSD_HEADS
    c_dt = D_SSD + CONV_CH
    w_main = jnp.concatenate([w_in[:, :c_dt], w_in[:, c_dt + n_dt:]], axis=1).astype(bf16)
    w_dt = jnp.pad(w_in[:, c_dt:c_dt + n_dt], ((0, 0), (0, LANES - n_dt))).astype(bf16)
    proj, dt = _in_projection(x2, w_main, w_dt)
    proj3 = proj.reshape(bsz, seq, PROJ_MAIN)
    dt3 = dt.reshape(bsz, seq, LANES)

    conv_w8 = jnp.pad(conv_w.astype(f32), ((0, 8 - CONV_W), (0, 0)))
    xconv = _conv_silu(proj3, conv_w8, conv_b.astype(f32)[None, :])
    bias_row = _pad_lanes(jnp.concatenate([dt_bias_f, dt_bias_b]))
    a_row = _pad_lanes(jnp.concatenate([-jnp.exp(a_log_f.astype(f32)), -jnp.exp(a_log_b.astype(f32))]))
    dskip_row = jnp.repeat(d_skip.astype(f32), SSD_HEAD_DIM)[None, :]
    y_ssd = _ssd(proj3, xconv, dt3, bias_row, a_row, dskip_row, _head_expansion(0), _head_expansion(N_SSD_HEADS),
                 ssd_norm_w.astype(f32)[None, :])

    slopes = 2.0 ** (-8.0 * jnp.arange(1, N_ATTN_HEADS + 1, dtype=f32) / N_ATTN_HEADS)
    slopes3 = jnp.broadcast_to(slopes[:, None, None], (N_ATTN_HEADS, 1, LANES))
    vec = lambda v: v.astype(f32)[None, :]
    y_attn = _diff_attention(proj3, slopes3, vec(lambda_q1), vec(lambda_k1), vec(lambda_q2), vec(lambda_k2),
                             vec(attn_norm_w), lambda_init)

    w_out_b = w_out.astype(bf16)
    h1, h1_packed, logits_t = _out_projection(
        y_ssd.reshape(n_tok, D_SSD), y_attn.reshape(n_tok, D_ATTN), x2, w_out_b[:D_SSD], w_out_b[D_SSD:],
        vec(ln1_g), vec(ln1_b), w_router.astype(f32).T)

    eidx, gate, rel, counts = _route(
        logits_t.reshape(N_EXPERT_GROUPS, EXPERTS_PER_GROUP, n_tok),
        router_bias.astype(f32).reshape(N_EXPERT_GROUPS, EXPERTS_PER_GROUP, 1))
    counts = counts[:, 0].astype(i32)
    padded = (counts + BLOCK_ROWS - 1) // BLOCK_ROWS * BLOCK_ROWS
    pends = jnp.cumsum(padded)
    pstarts = pends - padded
    n_blocks = -(-(n_tok * TOP_K) // BLOCK_ROWS) + N_EXPERTS
    n_used = (pends[-1] // BLOCK_ROWS).astype(i32)
    blk = jnp.minimum(jnp.arange(n_blocks, dtype=i32), n_used - 1)
    block_exp = jnp.minimum(jnp.searchsorted(pends, blk * BLOCK_ROWS, side='right'), N_EXPERTS - 1).astype(i32)
    dest = _dest_rows(eidx, rel, pstarts[:, None].astype(f32))

    x_sorted = _dispatch(dest, h1_packed, n_blocks * BLOCK_ROWS)
    y_sorted = _expert_ffn(block_exp, n_used[None], x_sorted, w_gate_e, w_up_e, w_down_e)
    out = _combine(dest, gate.T, h1, w_gate_s.astype(bf16), w_up_s.astype(bf16), w_down_s.astype(bf16),
                   vec(ln2_g), vec(ln2_b), y_sorted)
    return out.reshape(bsz, seq, d)


def kernel(x, w_in, conv_w, conv_b, dt_bias_f, dt_bias_b, a_log_f, a_log_b, d_skip, ssd_norm_w, lambda_q1, lambda_k1, lambda_q2, lambda_k2, attn_norm_w, w_out, ln1_g, ln1_b, w_router, router_bias, w_gate_e, w_up_e, w_down_e, w_gate_s, w_up_s, w_down_s, ln2_g, ln2_b):
    h = x
    for l in range(DEPTH):
        lambda_init = 0.8 - 0.6 * math.exp(-0.3 * l)
        h = _layer(h, w_in[l], conv_w[l], conv_b[l], dt_bias_f[l], dt_bias_b[l], a_log_f[l], a_log_b[l],
                   d_skip[l], ssd_norm_w[l], lambda_q1[l], lambda_k1[l], lambda_q2[l], lambda_k2[l],
                   attn_norm_w[l], w_out[l], ln1_g[l], ln1_b[l], w_router[l], router_bias[l],
                   w_gate_e[l], w_up_e[l], w_down_e[l], w_gate_s[l], w_up_s[l], w_down_s[l],
                   ln2_g[l], ln2_b[l], lambda_init)
    return h
```

```python
import functools
import math

import jax
import jax.numpy as jnp
from jax import lax
from jax.experimental import pallas as pl
from jax.experimental.pallas import tpu as pltpu

f32 = jnp.float32
bf16 = jnp.bfloat16
i32 = jnp.int32

D_MODEL = 2048
D_SSD = 1024
D_ATTN = 1024
SSD_HEAD_DIM = 64
N_SSD_HEADS = 16
SSD_GROUPS = 2
HEADS_PER_GROUP = N_SSD_HEADS // SSD_GROUPS
D_STATE = 128
CONV_W = 5
CONV_CH = D_SSD + 2 * SSD_GROUPS * D_STATE
CHUNK = 256
ATTN_HEAD_DIM = 64
N_ATTN_HEADS = 8
N_EXPERTS = 64
N_EXPERT_GROUPS = 8
EXPERTS_PER_GROUP = N_EXPERTS // N_EXPERT_GROUPS
TOPK_GROUPS = 4
TOP_K = 8
D_EXPERT = 512
ROUTED_SCALE = 2.5
BLOCK_ROWS = 512
LN_EPS = 1e-5
RMS_EPS = 1e-5
DEPTH = 1
ALPHA = (2 * DEPTH) ** 0.25

LANES = 128
BF16_SUBLANES = 16
VMEM_LIMIT = 56 * 1024 * 1024

PROJ_MAIN = D_SSD + CONV_CH + 3 * D_ATTN
COL_XBC = D_SSD
COL_Q = D_SSD + CONV_CH
COL_K = COL_Q + D_ATTN
COL_V = COL_K + D_ATTN

def _sigmoid(x):
    return 1.0 / (1.0 + jnp.exp(-x))


def _silu(x):
    return x * _sigmoid(x)


def _softplus(x):
    return jnp.maximum(x, 0.0) + jnp.log(1.0 + jnp.exp(-jnp.abs(x)))


def _layer_norm(r, g, b):
    mu = jnp.mean(r, axis=-1, keepdims=True)
    c = r - mu
    var = jnp.mean(c * c, axis=-1, keepdims=True)
    return c * lax.rsqrt(var + LN_EPS) * g + b


def _params(*sem):
    return pltpu.CompilerParams(dimension_semantics=sem, vmem_limit_bytes=VMEM_LIMIT)


def _inproj_kernel(x_ref, w_ref, wdt_ref, o_ref, dt_ref, xb_ref):
    @pl.when(pl.program_id(1) == 0)
    def _():
        xb_ref[...] = x_ref[...].astype(bf16)
        dt_ref[...] = jnp.dot(xb_ref[...], wdt_ref[...], preferred_element_type=f32)

    o_ref[...] = jnp.dot(xb_ref[...], w_ref[...], preferred_element_type=f32).astype(bf16)


def _in_projection(x2, w_main, w_dt):
    n_tok = x2.shape[0]
    tm = min(1024, n_tok)
    tn = 512
    return pl.pallas_call(
        _inproj_kernel,
        grid=(n_tok // tm, PROJ_MAIN // tn),
        in_specs=[
            pl.BlockSpec((tm, D_MODEL), lambda i, j: (i, 0)),
            pl.BlockSpec((D_MODEL, tn), lambda i, j: (0, j)),
            pl.BlockSpec((D_MODEL, LANES), lambda i, j: (0, 0)),
        ],
        out_specs=[
            pl.BlockSpec((tm, tn), lambda i, j: (i, j)),
            pl.BlockSpec((tm, LANES), lambda i, j: (i, 0)),
        ],
        out_shape=[
            jax.ShapeDtypeStruct((n_tok, PROJ_MAIN), bf16),
            jax.ShapeDtypeStruct((n_tok, LANES), f32),
        ],
        scratch_shapes=[pltpu.VMEM((tm, D_MODEL), bf16)],
        compiler_params=_params("parallel", "arbitrary"),
        name="in_projection",
    )(x2, w_main, w_dt)


CONV_HALO = 16


def _conv_kernel(x_ref, w_ref, b_ref, o_ref, pad_ref):
    seq = x_ref.shape[1]
    tc = x_ref.shape[2]
    zeros = jnp.zeros((CONV_HALO, tc), f32)
    pad_ref[0:CONV_HALO, :] = zeros
    pad_ref[seq + CONV_HALO:seq + 2 * CONV_HALO, :] = zeros
    w = w_ref[...]
    b = b_ref[...]
    win = CHUNK + 2 * CONV_HALO

    def fill(i, carry):
        base = pl.multiple_of(i * CHUNK, CHUNK)
        pad_ref[pl.ds(base + CONV_HALO, CHUNK), :] = x_ref[0, pl.ds(base, CHUNK), :].astype(f32)
        return carry

    lax.fori_loop(0, seq // CHUNK, fill, 0)

    def body(i, carry):
        base = pl.multiple_of(i * CHUNK, CHUNK)
        xe = pad_ref[pl.ds(base, win), :]
        acc = jnp.zeros((CHUNK, tc), f32) + b
        for k in range(CONV_W):
            d = k - CONV_W // 2
            r = xe if d == 0 else pltpu.roll(xe, (-d) % win, 0)
            acc = acc + r[CONV_HALO:CONV_HALO + CHUNK, :] * w[k:k + 1, :]
        o_ref[0, pl.ds(base, CHUNK), :] = _silu(acc).astype(bf16)
        return carry

    lax.fori_loop(0, seq // CHUNK, body, 0)


def _conv_silu(proj3, conv_w8, conv_b2):
    bsz, seq, _ = proj3.shape
    tc = 256
    col0 = COL_XBC // tc
    return pl.pallas_call(
        _conv_kernel,
        grid=(bsz, CONV_CH // tc),
        in_specs=[
            pl.BlockSpec((1, seq, tc), lambda b, j: (b, 0, col0 + j)),
            pl.BlockSpec((8, tc), lambda b, j: (0, j)),
            pl.BlockSpec((1, tc), lambda b, j: (0, j)),
        ],
        out_specs=pl.BlockSpec((1, seq, tc), lambda b, j: (b, 0, j)),
        out_shape=jax.ShapeDtypeStruct((bsz, seq, CONV_CH), bf16),
        scratch_shapes=[pltpu.VMEM((seq + 2 * CONV_HALO, tc), f32)],
        compiler_params=_params("parallel", "parallel"),
        name="conv_silu",
    )(proj3, conv_w8, conv_b2)


def _ssd_cumsums(dt_ref, bias_ref, arow_ref):
    dtv = _softplus(dt_ref[0] + bias_ref[...])
    adt = dtv * arow_ref[...]
    li = lax.broadcasted_iota(i32, (CHUNK, CHUNK), 0)
    si = lax.broadcasted_iota(i32, (CHUNK, CHUNK), 1)
    tril = (si <= li).astype(bf16)
    triu = (si >= li).astype(bf16)
    lane = lax.broadcasted_iota(i32, (CHUNK, LANES), 1)
    hi = adt.astype(bf16)
    r1 = adt - hi.astype(f32)
    mid = r1.astype(bf16)
    lo = (r1 - mid.astype(f32)).astype(bf16)
    parts = jnp.concatenate([hi, mid, lo], axis=1)

    def tri_sum(tri):
        t = jnp.dot(tri, parts, preferred_element_type=f32)
        return t[:, :LANES] + t[:, LANES:2 * LANES] + t[:, 2 * LANES:]

    cs = jnp.where(lane < N_SSD_HEADS, tri_sum(tril), tri_sum(triu))
    return dtv, cs


def _expand_heads(v, ex_ref):
    hi = v.astype(bf16)
    lo = (v - hi.astype(f32)).astype(bf16)
    return jnp.dot(jnp.concatenate([hi, lo], axis=1), ex_ref[...], preferred_element_type=f32)


def _state_update(h_ref, bc, xw, decay_row, b_col0):
    gw = HEADS_PER_GROUP * SSD_HEAD_DIM
    new = []
    for g in range(SSD_GROUPS):
        bg = bc[:, b_col0 + g * D_STATE:b_col0 + (g + 1) * D_STATE]
        bgt = bg.astype(f32).T.astype(bf16)
        new.append(jnp.dot(bgt, xw[:, g * gw:(g + 1) * gw], preferred_element_type=f32))
    h_ref[...] = h_ref[...] * decay_row + jnp.concatenate(new, axis=1)


def _state_readout(h_ref, bc, c_col0):
    gw = HEADS_PER_GROUP * SSD_HEAD_DIM
    outs = []
    for g in range(SSD_GROUPS):
        cg = bc[:, c_col0 + g * D_STATE:c_col0 + (g + 1) * D_STATE]
        outs.append(jnp.dot(cg, h_ref[:, g * gw:(g + 1) * gw].astype(bf16), preferred_element_type=f32))
    return jnp.concatenate(outs, axis=1)


C_COL0 = SSD_GROUPS * D_STATE


def _ssd_fwd_kernel(x_ref, bc_ref, dt_ref, bias_ref, arow_ref, dskip_ref, exf_ref, y_ref, h_ref):
    @pl.when(pl.program_id(1) == 0)
    def _():
        h_ref[...] = jnp.zeros_like(h_ref)

    x = x_ref[0]
    bc = bc_ref[0]
    dtv, cs = _ssd_cumsums(dt_ref, bias_ref, arow_ref)
    cst = cs.T
    dtt = dtv.T
    li = lax.broadcasted_iota(i32, (CHUNK, CHUNK), 0)
    si = lax.broadcasted_iota(i32, (CHUNK, CHUNK), 1)
    lower = li > si
    upper = li < si
    lower_eq = li >= si
    lane = lax.broadcasted_iota(i32, (CHUNK, LANES), 1)
    first_half = lane < SSD_HEAD_DIM

    cb = []
    for g in range(SSD_GROUPS):
        bg = bc[:, g * D_STATE:(g + 1) * D_STATE]
        cg = bc[:, C_COL0 + g * D_STATE:C_COL0 + (g + 1) * D_STATE]
        cb.append(lax.dot_general(cg, bg, (((1,), (1,)), ((), ())), preferred_element_type=f32))

    pairs = []
    for j in range(N_SSD_HEADS // 2):
        xpair = x[:, j * LANES:(j + 1) * LANES]
        acc = None
        for h, xm in ((2 * j, jnp.where(first_half, xpair, jnp.zeros_like(xpair))),
                      (2 * j + 1, jnp.where(first_half, jnp.zeros_like(xpair), xpair))):
            hb = N_SSD_HEADS + h
            arg = jnp.where(lower_eq, cs[:, h:h + 1] - cst[h:h + 1, :], cs[:, hb:hb + 1] - cst[hb:hb + 1, :])
            dtf = dtt[h:h + 1, :]
            dtb = dtt[hb:hb + 1, :]
            wgt = jnp.where(lower, dtf, jnp.where(upper, dtb, dtf + dtb))
            m = (cb[h // HEADS_PER_GROUP] * jnp.exp(arg) * wgt).astype(bf16)
            t = jnp.dot(m, xm, preferred_element_type=f32)
            acc = t if acc is None else acc + t
        pairs.append(acc)
    y = jnp.concatenate(pairs, axis=1)

    xf = x.astype(f32)
    e_f = _expand_heads(jnp.exp(cs), exf_ref)
    y = y + _state_readout(h_ref, bc, C_COL0) * e_f + dskip_ref[...] * xf
    y_ref[0] = y

    to_end = jnp.where(lane < N_SSD_HEADS, cs[CHUNK - 1:CHUNK, :] - cs, 0.0)
    w_f = _expand_heads(jnp.exp(to_end) * dtv, exf_ref)
    _state_update(h_ref, bc, (xf * w_f).astype(bf16), e_f[CHUNK - 1:CHUNK, :], 0)


def _ssd_bwd_kernel(x_ref, bc_ref, dt_ref, z_ref, yp_ref, bias_ref, arow_ref, exb_ref, nw_ref, o_ref, h_ref):
    @pl.when(pl.program_id(1) == 0)
    def _():
        h_ref[...] = jnp.zeros_like(h_ref)

    x = x_ref[0]
    bc = bc_ref[0]
    dtv, cs = _ssd_cumsums(dt_ref, bias_ref, arow_ref)
    e_b = _expand_heads(jnp.exp(cs), exb_ref)
    y = yp_ref[0] + _state_readout(h_ref, bc, C_COL0) * e_b

    lane = lax.broadcasted_iota(i32, (CHUNK, LANES), 1)
    to_start = jnp.where((lane >= N_SSD_HEADS) & (lane < 2 * N_SSD_HEADS), cs[0:1, :] - cs, 0.0)
    w_b = _expand_heads(jnp.exp(to_start) * dtv, exb_ref)
    _state_update(h_ref, bc, (x.astype(f32) * w_b).astype(bf16), e_b[0:1, :], 0)

    y = y * _silu(z_ref[0].astype(f32))
    gw = D_SSD // SSD_GROUPS
    outs = []
    for g in range(SSD_GROUPS):
        yg = y[:, g * gw:(g + 1) * gw]
        outs.append(yg * lax.rsqrt(jnp.mean(yg * yg, axis=-1, keepdims=True) + RMS_EPS))
    o_ref[0] = (jnp.concatenate(outs, axis=1) * nw_ref[...]).astype(bf16)


def _ssd(proj3, xconv, dt3, bias_row, a_row, dskip_row, ex_f, ex_b, norm_w_row):
    bsz, seq, _ = proj3.shape
    nc = seq // CHUNK
    bc_blk = D_SSD // (2 * SSD_GROUPS * D_STATE)
    row = lambda n: pl.BlockSpec((1, n), lambda b, c: (0, 0))
    ex_spec = pl.BlockSpec((2 * LANES, D_SSD), lambda b, c: (0, 0))
    fwd = lambda b, c: (b, c, 0)
    y_part = pl.pallas_call(
        _ssd_fwd_kernel,
        grid=(bsz, nc),
        in_specs=[
            pl.BlockSpec((1, CHUNK, D_SSD), fwd),
            pl.BlockSpec((1, CHUNK, 2 * SSD_GROUPS * D_STATE), lambda b, c: (b, c, bc_blk)),
            pl.BlockSpec((1, CHUNK, LANES), fwd),
            row(LANES), row(LANES), row(D_SSD), ex_spec,
        ],
        out_specs=pl.BlockSpec((1, CHUNK, D_SSD), fwd),
        out_shape=jax.ShapeDtypeStruct((bsz, seq, D_SSD), f32),
        scratch_shapes=[pltpu.VMEM((D_STATE, D_SSD), f32)],
        compiler_params=_params("parallel", "arbitrary"),
        name="ssd_forward_sweep",
    )(xconv, xconv, dt3, bias_row, a_row, dskip_row, ex_f)

    rev = lambda b, c: (b, nc - 1 - c, 0)
    return pl.pallas_call(
        _ssd_bwd_kernel,
        grid=(bsz, nc),
        in_specs=[
            pl.BlockSpec((1, CHUNK, D_SSD), rev),
            pl.BlockSpec((1, CHUNK, 2 * SSD_GROUPS * D_STATE), lambda b, c: (b, nc - 1 - c, bc_blk)),
            pl.BlockSpec((1, CHUNK, LANES), rev),
            pl.BlockSpec((1, CHUNK, D_SSD), rev),
            pl.BlockSpec((1, CHUNK, D_SSD), rev),
            row(LANES), row(LANES), ex_spec, row(D_SSD),
        ],
        out_specs=pl.BlockSpec((1, CHUNK, D_SSD), rev),
        out_shape=jax.ShapeDtypeStruct((bsz, seq, D_SSD), bf16),
        scratch_shapes=[pltpu.VMEM((D_STATE, D_SSD), f32)],
        compiler_params=_params("parallel", "arbitrary"),
        name="ssd_backward_sweep",
    )(xconv, xconv, dt3, proj3, y_part, bias_row, a_row, ex_b, norm_w_row)


ATTN_TQ = 256


def _attn_kernel(q_ref, k_ref, v_ref, slope_ref, lq1_ref, lk1_ref, lq2_ref, lk2_ref, nw_ref, o_ref, *, lambda_init):
    q = q_ref[0]
    k = k_ref[0]
    v = v_ref[0]
    tq = q.shape[0]
    seq = k.shape[0]
    lam = (jnp.exp(jnp.sum(lq1_ref[...] * lk1_ref[...], axis=-1, keepdims=True))
           - jnp.exp(jnp.sum(lq2_ref[...] * lk2_ref[...], axis=-1, keepdims=True)) + lambda_init)
    slope = slope_ref[0][:, 0:1]
    qpos = (pl.program_id(2) * tq + lax.broadcasted_iota(i32, (tq, seq), 0)).astype(f32)
    kpos = lax.broadcasted_iota(i32, (tq, seq), 1).astype(f32)
    bias = -slope * jnp.abs(qpos - kpos)
    lane = lax.broadcasted_iota(i32, q.shape, 1)
    first_half = lane < ATTN_HEAD_DIM
    zero = jnp.zeros_like(q)
    scale = ATTN_HEAD_DIM ** -0.5

    def softmax_map(qm):
        s = lax.dot_general(qm, k, (((1,), (1,)), ((), ())), preferred_element_type=f32) * scale + bias
        p = jnp.exp(s - jnp.max(s, axis=-1, keepdims=True))
        return p / jnp.sum(p, axis=-1, keepdims=True)

    att = softmax_map(jnp.where(first_half, q, zero)) - lam * softmax_map(jnp.where(first_half, zero, q))
    o = jnp.dot(att.astype(bf16), v, preferred_element_type=f32)
    o = o * lax.rsqrt(jnp.mean(o * o, axis=-1, keepdims=True) + RMS_EPS) * nw_ref[...] * (1.0 - lambda_init)
    o_ref[0] = o.astype(bf16)


def _diff_attention(proj3, slopes3, lq1, lk1, lq2, lk2, norm_w_row, lambda_init):
    bsz, seq, _ = proj3.shape
    tq = min(ATTN_TQ, seq)
    hd = 2 * ATTN_HEAD_DIM
    qb, kb, vb = COL_Q // hd, COL_K // hd, COL_V // hd
    vec = lambda n: pl.BlockSpec((1, n), lambda b, h, i: (0, 0))
    return pl.pallas_call(
        functools.partial(_attn_kernel, lambda_init=lambda_init),
        grid=(bsz, N_ATTN_HEADS, seq // tq),
        in_specs=[
            pl.BlockSpec((1, tq, hd), lambda b, h, i: (b, i, qb + h)),
            pl.BlockSpec((1, seq, hd), lambda b, h, i: (b, 0, kb + h)),
            pl.BlockSpec((1, seq, hd), lambda b, h, i: (b, 0, vb + h)),
            pl.BlockSpec((1, 1, LANES), lambda b, h, i: (h, 0, 0)),
            vec(ATTN_HEAD_DIM), vec(ATTN_HEAD_DIM), vec(ATTN_HEAD_DIM), vec(ATTN_HEAD_DIM), vec(hd),
        ],
        out_specs=pl.BlockSpec((1, tq, hd), lambda b, h, i: (b, i, h)),
        out_shape=jax.ShapeDtypeStruct((bsz, seq, D_ATTN), bf16),
        compiler_params=_params("parallel", "parallel", "arbitrary"),
        name="diff_attention",
    )(proj3, proj3, proj3, slopes3, lq1, lk1, lq2, lk2, norm_w_row)


def _outproj_kernel(ys_ref, ya_ref, x_ref, wa_ref, wb_ref, g_ref, b_ref, wr1_ref, wr2_ref, h_ref, lg_ref):
    mix = (jnp.dot(ys_ref[...], wa_ref[...], preferred_element_type=f32)
           + jnp.dot(ya_ref[...], wb_ref[...], preferred_element_type=f32))
    h = _layer_norm(ALPHA * x_ref[...] + mix, g_ref[...], b_ref[...])
    h_ref[...] = h
    h_hi = h.astype(bf16)
    h_lo = (h - h_hi.astype(f32)).astype(bf16)
    lg = (jnp.dot(h_hi, wr1_ref[...], preferred_element_type=f32)
          + jnp.dot(h_lo, wr2_ref[...], preferred_element_type=f32))
    lgt = lg.T
    lg_ref[...] = lgt[:N_EXPERTS] + lgt[N_EXPERTS:]


def _out_projection(y_ssd2, y_attn2, x2, w_a, w_b, g_row, b_row, w_router):
    n_tok = x2.shape[0]
    tm = min(512, n_tok)
    const = lambda shape: pl.BlockSpec(shape, lambda i: (0, 0), pipeline_mode=pl.Buffered(1))
    w_hi = w_router.astype(bf16)
    w_lo = (w_router - w_hi.astype(f32)).astype(bf16)
    wr1 = jnp.concatenate([w_hi, w_lo], axis=1)
    wr2 = jnp.concatenate([w_hi, jnp.zeros_like(w_hi)], axis=1)
    return pl.pallas_call(
        _outproj_kernel,
        grid=(n_tok // tm,),
        in_specs=[
            pl.BlockSpec((tm, D_SSD), lambda i: (i, 0)),
            pl.BlockSpec((tm, D_ATTN), lambda i: (i, 0)),
            pl.BlockSpec((tm, D_MODEL), lambda i: (i, 0)),
            const((D_SSD, D_MODEL)), const((D_ATTN, D_MODEL)),
            const((1, D_MODEL)), const((1, D_MODEL)),
            const((D_MODEL, 2 * N_EXPERTS)), const((D_MODEL, 2 * N_EXPERTS)),
        ],
        out_specs=[
            pl.BlockSpec((tm, D_MODEL), lambda i: (i, 0)),
            pl.BlockSpec((N_EXPERTS, tm), lambda i: (0, i)),
        ],
        out_shape=[
            jax.ShapeDtypeStruct((n_tok, D_MODEL), f32),
            jax.ShapeDtypeStruct((N_EXPERTS, n_tok), f32),
        ],
        compiler_params=_params("parallel"),
        name="out_projection_ln",
    )(y_ssd2, y_attn2, x2, w_a, w_b, g_row, b_row, wr1, wr2)


ROUTE_TT = 512


def _route_kernel(lg_ref, bias_ref, eidx_ref, gate_ref, rel_ref, cnt_ref, run_ref):
    @pl.when(pl.program_id(0) == 0)
    def _():
        run_ref[...] = jnp.zeros_like(run_ref)

    tt = lg_ref.shape[2]
    shape3 = (N_EXPERT_GROUPS, EXPERTS_PER_GROUP, tt)
    neg = -jnp.inf
    sc = _sigmoid(lg_ref[...])
    bi = sc + bias_ref[...]
    gidx = lax.broadcasted_iota(i32, shape3, 0).astype(f32)
    jidx = lax.broadcasted_iota(i32, shape3, 1).astype(f32)
    eid = gidx * EXPERTS_PER_GROUP + jidx

    m1 = jnp.max(bi, axis=1, keepdims=True)
    i1 = jnp.min(jnp.where(bi == m1, jidx, float(EXPERTS_PER_GROUP)), axis=1, keepdims=True)
    m2 = jnp.max(jnp.where(jidx == i1, neg, bi), axis=1, keepdims=True)
    gs = m1 + m2
    gcol = lax.broadcasted_iota(i32, gs.shape, 0).astype(f32)
    gmask = jnp.zeros(gs.shape, f32)
    for _ in range(TOPK_GROUPS):
        m = jnp.max(gs, axis=0, keepdims=True)
        ig = jnp.min(jnp.where(gs == m, gcol, float(N_EXPERT_GROUPS)), axis=0, keepdims=True)
        pick = gcol == ig
        gmask = jnp.where(pick, 1.0, gmask)
        gs = jnp.where(pick, neg, gs)

    val = jnp.where(gmask > 0.0, bi, neg)
    sel = jnp.zeros(shape3, f32)
    gates = []
    for k in range(TOP_K):
        m = jnp.max(jnp.max(val, axis=1, keepdims=True), axis=0, keepdims=True)
        ie = jnp.min(jnp.min(jnp.where(val == m, eid, float(N_EXPERTS)), axis=1, keepdims=True), axis=0, keepdims=True)
        pick = eid == ie
        gates.append(jnp.sum(jnp.sum(jnp.where(pick, sc, 0.0), axis=1, keepdims=True), axis=0, keepdims=True)[0])
        eidx_ref[k:k + 1, :] = ie[0].astype(i32)
        sel = jnp.where(pick, 1.0, sel)
        val = jnp.where(pick, neg, val)
    gsum = gates[0]
    for k in range(1, TOP_K):
        gsum = gsum + gates[k]
    for k in range(TOP_K):
        gate_ref[k:k + 1, :] = gates[k] / gsum * ROUTED_SCALE

    sel2 = sel.reshape(N_EXPERTS, tt)
    s_i = lax.broadcasted_iota(i32, (tt, tt), 0)
    t_i = lax.broadcasted_iota(i32, (tt, tt), 1)
    before = (s_i < t_i).astype(bf16)
    pos = jnp.dot(sel2.astype(bf16), before, preferred_element_type=f32) + run_ref[...]
    erow = lax.broadcasted_iota(i32, (N_EXPERTS, tt), 0)
    for k in range(TOP_K):
        pick2 = erow == eidx_ref[k:k + 1, :]
        rel_ref[k:k + 1, :] = jnp.sum(jnp.where(pick2, pos, 0.0), axis=0, keepdims=True).astype(i32)
    run_ref[...] = run_ref[...] + jnp.sum(sel2, axis=1, keepdims=True)
    cnt_ref[...] = jnp.broadcast_to(run_ref[...], cnt_ref.shape)


def _route(logits3, bias3):
    n_tok = logits3.shape[2]
    tt = min(ROUTE_TT, n_tok)
    tok_blk = pl.BlockSpec((TOP_K, tt), lambda i: (0, i))
    return pl.pallas_call(
        _route_kernel,
        grid=(n_tok // tt,),
        in_specs=[
            pl.BlockSpec((N_EXPERT_GROUPS, EXPERTS_PER_GROUP, tt), lambda i: (0, 0, i)),
            pl.BlockSpec((N_EXPERT_GROUPS, EXPERTS_PER_GROUP, 1), lambda i: (0, 0, 0)),
        ],
        out_specs=[tok_blk, tok_blk, tok_blk, pl.BlockSpec((N_EXPERTS, LANES), lambda i: (0, 0))],
        out_shape=[
            jax.ShapeDtypeStruct((TOP_K, n_tok), i32),
            jax.ShapeDtypeStruct((TOP_K, n_tok), f32),
            jax.ShapeDtypeStruct((TOP_K, n_tok), i32),
            jax.ShapeDtypeStruct((N_EXPERTS, LANES), f32),
        ],
        scratch_shapes=[pltpu.VMEM((N_EXPERTS, 1), f32)],
        compiler_params=_params("arbitrary"),
        name="route_topk",
    )(logits3, bias3)


def _dest_kernel(eidx_ref, rel_ref, pstart_ref, dest_ref):
    tt = eidx_ref.shape[1]
    erow = lax.broadcasted_iota(i32, (N_EXPERTS, tt), 0)
    pstart = pstart_ref[...]
    for k in range(TOP_K):
        pick = erow == eidx_ref[k:k + 1, :]
        base = jnp.sum(jnp.where(pick, pstart, 0.0), axis=0, keepdims=True)
        dest_ref[k:k + 1, :] = rel_ref[k:k + 1, :] + base.astype(i32)


def _dest_rows(eidx, rel, pstart_col):
    n_tok = eidx.shape[1]
    tt = min(ROUTE_TT, n_tok)
    tok_blk = pl.BlockSpec((TOP_K, tt), lambda i: (0, i))
    return pl.pallas_call(
        _dest_kernel,
        grid=(n_tok // tt,),
        in_specs=[tok_blk, tok_blk, pl.BlockSpec((N_EXPERTS, 1), lambda i: (0, 0))],
        out_specs=tok_blk,
        out_shape=jax.ShapeDtypeStruct((TOP_K, n_tok), i32),
        compiler_params=_params("parallel"),
        name="dest_rows",
    )(eidx, rel, pstart_col)


MOE_TM = 256


def _dispatch_kernel(dest_ref, h_ref, xs_ref, sem):
    tm = h_ref.shape[0]

    def issue(t, carry):
        for k in range(TOP_K):
            pltpu.make_async_copy(h_ref.at[pl.ds(t, 1)], xs_ref.at[pl.ds(dest_ref[k, t], 1)], sem).start()
        return carry

    lax.fori_loop(0, tm, issue, 0)
    for _ in range(TOP_K):
        pltpu.make_async_copy(h_ref, xs_ref.at[pl.ds(0, tm)], sem).wait()


def _dispatch(dest, h1, n_rows):
    n_tok = h1.shape[0]
    tm = min(MOE_TM, n_tok)
    return pl.pallas_call(
        _dispatch_kernel,
        grid=(n_tok // tm,),
        in_specs=[
            pl.BlockSpec((TOP_K, tm), lambda i: (0, i), memory_space=pltpu.SMEM),
            pl.BlockSpec((tm, D_MODEL), lambda i: (i, 0)),
        ],
        out_specs=pl.BlockSpec(memory_space=pl.ANY),
        out_shape=jax.ShapeDtypeStruct((n_rows, D_MODEL), f32),
        scratch_shapes=[pltpu.SemaphoreType.DMA(())],
        compiler_params=_params("arbitrary"),
        name="moe_dispatch",
    )(dest, h1)


def _expert_kernel(bexp_ref, nused_ref, next_ref, x_ref, wg_hbm, wu_hbm, wd_hbm, o_ref,
                   stage_g, stage_u, stage_d, wg_s, wu_s, wd_s, sem):
    i = pl.program_id(0)
    e = bexp_ref[i]
    first_block_of_expert = (i == 0) | (e != bexp_ref[jnp.maximum(i - 1, 0)])

    def weight_copies(expert):
        return (pltpu.make_async_copy(wg_hbm.at[expert], stage_g, sem.at[0]),
                pltpu.make_async_copy(wu_hbm.at[expert], stage_u, sem.at[1]),
                pltpu.make_async_copy(wd_hbm.at[expert], stage_d, sem.at[2]))

    @pl.when(i == 0)
    def _():
        for c in weight_copies(e):
            c.start()

    @pl.when(first_block_of_expert)
    def _():
        for c in weight_copies(e):
            c.wait()
        wg_s[...] = stage_g[...].astype(bf16)
        wu_s[...] = stage_u[...].astype(bf16)
        wd_s[...] = stage_d[...].astype(bf16)

        @pl.when(next_ref[i] >= 0)
        def _():
            for c in weight_copies(next_ref[i]):
                c.start()

    @pl.when(i < nused_ref[0])
    def _():
        xb = x_ref[...].astype(bf16)
        hg = jnp.dot(xb, wg_s[...], preferred_element_type=f32)
        hu = jnp.dot(xb, wu_s[...], preferred_element_type=f32)
        hb = (_silu(hg) * hu).astype(bf16)
        o_ref[...] = jnp.dot(hb, wd_s[...], preferred_element_type=f32)


def _expert_ffn(block_exp, n_used, next_exp, x_sorted, w_gate_e, w_up_e, w_down_e):
    n_blocks = x_sorted.shape[0] // BLOCK_ROWS
    rows = lambda i, be, nu, nx: (jnp.minimum(i, nu[0] - 1), 0)
    blk = pl.BlockSpec((BLOCK_ROWS, D_MODEL), rows)
    hbm = pl.BlockSpec(memory_space=pl.ANY)
    return pl.pallas_call(
        _expert_kernel,
        grid_spec=pltpu.PrefetchScalarGridSpec(
            num_scalar_prefetch=3,
            grid=(n_blocks,),
            in_specs=[blk, hbm, hbm, hbm],
            out_specs=blk,
            scratch_shapes=[
                pltpu.VMEM((D_MODEL, D_EXPERT), f32),
                pltpu.VMEM((D_MODEL, D_EXPERT), f32),
                pltpu.VMEM((D_EXPERT, D_MODEL), f32),
                pltpu.VMEM((D_MODEL, D_EXPERT), bf16),
                pltpu.VMEM((D_MODEL, D_EXPERT), bf16),
                pltpu.VMEM((D_EXPERT, D_MODEL), bf16),
                pltpu.SemaphoreType.DMA((3,)),
            ],
        ),
        out_shape=jax.ShapeDtypeStruct(x_sorted.shape, f32),
        compiler_params=_params("arbitrary"),
        name="moe_expert_ffn",
    )(block_exp, n_used, next_exp, x_sorted, w_gate_e, w_up_e, w_down_e)


def _combine_kernel(dest_ref, gate_ref, h_ref, wg_ref, wu_ref, wd_ref, g_ref, b_ref, ys_ref, o_ref, buf, sem):
    tm = h_ref.shape[0]

    def issue(t, carry):
        for k in range(TOP_K):
            pltpu.make_async_copy(ys_ref.at[pl.ds(dest_ref[k, t], 1)], buf.at[k, pl.ds(t, 1)], sem).start()
        return carry

    lax.fori_loop(0, tm, issue, 0)

    h = h_ref[...]
    hb = h.astype(bf16)
    hg = jnp.dot(hb, wg_ref[...], preferred_element_type=f32)
    hu = jnp.dot(hb, wu_ref[...], preferred_element_type=f32)
    ffn = jnp.dot((_silu(hg) * hu).astype(bf16), wd_ref[...], preferred_element_type=f32)

    for k in range(TOP_K):
        pltpu.make_async_copy(ys_ref.at[pl.ds(0, tm)], buf.at[k], sem).wait()
    gate = gate_ref[...]
    for k in range(TOP_K):
        ffn = ffn + gate[:, k:k + 1] * buf[k]
    o_ref[...] = _layer_norm(ALPHA * h + ffn, g_ref[...], b_ref[...])


def _combine(dest, gate_t, h1, w_gate_s, w_up_s, w_down_s, g_row, b_row, y_sorted):
    n_tok = h1.shape[0]
    tm = min(MOE_TM, n_tok)
    const = lambda shape: pl.BlockSpec(shape, lambda i: (0, 0), pipeline_mode=pl.Buffered(1))
    return pl.pallas_call(
        _combine_kernel,
        grid=(n_tok // tm,),
        in_specs=[
            pl.BlockSpec((TOP_K, tm), lambda i: (0, i), memory_space=pltpu.SMEM),
            pl.BlockSpec((tm, TOP_K), lambda i: (i, 0)),
            pl.BlockSpec((tm, D_MODEL), lambda i: (i, 0)),
            const((D_MODEL, D_EXPERT)), const((D_MODEL, D_EXPERT)), const((D_EXPERT, D_MODEL)),
            const((1, D_MODEL)), const((1, D_MODEL)),
            pl.BlockSpec(memory_space=pl.ANY),
        ],
        out_specs=pl.BlockSpec((tm, D_MODEL), lambda i: (i, 0)),
        out_shape=jax.ShapeDtypeStruct((n_tok, D_MODEL), f32),
        scratch_shapes=[pltpu.VMEM((TOP_K, tm, D_MODEL), f32), pltpu.SemaphoreType.DMA(())],
        compiler_params=_params("arbitrary"),
        name="moe_combine_ln",
    )(dest, gate_t, h1, w_gate_s, w_up_s, w_down_s, g_row, b_row, y_sorted)


def _head_expansion(first_lane):
    r = jnp.arange(LANES)[:, None]
    c = jnp.arange(D_SSD)[None, :] // SSD_HEAD_DIM
    ex = (r == c + first_lane).astype(bf16)
    return jnp.concatenate([ex, ex], axis=0)


def _pad_lanes(v):
    return jnp.pad(v.astype(f32), (0, LANES - v.shape[0]))[None, :]


def _layer(h3, w_in, conv_w, conv_b, dt_bias_f, dt_bias_b, a_log_f, a_log_b, d_skip, ssd_norm_w,
           lambda_q1, lambda_k1, lambda_q2, lambda_k2, attn_norm_w, w_out, ln1_g, ln1_b,
           w_router, router_bias, w_gate_e, w_up_e, w_down_e, w_gate_s, w_up_s, w_down_s,
           ln2_g, ln2_b, lambda_init):
    bsz, seq, d = h3.shape
    n_tok = bsz * seq
    x2 = h3.reshape(n_tok, d)

    n_dt = 2 * N_SSD_HEADS
    c_dt = D_SSD + CONV_CH
    w_main = jnp.concatenate([w_in[:, :c_dt], w_in[:, c_dt + n_dt:]], axis=1).astype(bf16)
    w_dt = jnp.pad(w_in[:, c_dt:c_dt + n_dt], ((0, 0), (0, LANES - n_dt))).astype(bf16)
    proj, dt = _in_projection(x2, w_main, w_dt)
    proj3 = proj.reshape(bsz, seq, PROJ_MAIN)
    dt3 = dt.reshape(bsz, seq, LANES)

    conv_w8 = jnp.pad(conv_w.astype(f32), ((0, 8 - CONV_W), (0, 0)))
    xconv = _conv_silu(proj3, conv_w8, conv_b.astype(f32)[None, :])
    bias_row = _pad_lanes(jnp.concatenate([dt_bias_f, dt_bias_b]))
    a_row = _pad_lanes(jnp.concatenate([-jnp.exp(a_log_f.astype(f32)), -jnp.exp(a_log_b.astype(f32))]))
    dskip_row = jnp.repeat(d_skip.astype(f32), SSD_HEAD_DIM)[None, :]
    y_ssd = _ssd(proj3, xconv, dt3, bias_row, a_row, dskip_row, _head_expansion(0), _head_expansion(N_SSD_HEADS),
                 ssd_norm_w.astype(f32)[None, :])

    slopes = 2.0 ** (-8.0 * jnp.arange(1, N_ATTN_HEADS + 1, dtype=f32) / N_ATTN_HEADS)
    slopes3 = jnp.broadcast_to(slopes[:, None, None], (N_ATTN_HEADS, 1, LANES))
    vec = lambda v: v.astype(f32)[None, :]
    y_attn = _diff_attention(proj3, slopes3, vec(lambda_q1), vec(lambda_k1), vec(lambda_q2), vec(lambda_k2),
                             vec(attn_norm_w), lambda_init)

    w_out_b = w_out.astype(bf16)
    h1, logits_t = _out_projection(
        y_ssd.reshape(n_tok, D_SSD), y_attn.reshape(n_tok, D_ATTN), x2, w_out_b[:D_SSD], w_out_b[D_SSD:],
        vec(ln1_g), vec(ln1_b), w_router.astype(f32))

    eidx, gate, rel, counts = _route(
        logits_t.reshape(N_EXPERT_GROUPS, EXPERTS_PER_GROUP, n_tok),
        router_bias.astype(f32).reshape(N_EXPERT_GROUPS, EXPERTS_PER_GROUP, 1))
    counts = counts[:, 0].astype(i32)
    padded = (counts + BLOCK_ROWS - 1) // BLOCK_ROWS * BLOCK_ROWS
    pends = jnp.cumsum(padded)
    pstarts = pends - padded
    n_blocks = -(-(n_tok * TOP_K) // BLOCK_ROWS) + N_EXPERTS
    n_used = (pends[-1] // BLOCK_ROWS).astype(i32)
    blk = jnp.minimum(jnp.arange(n_blocks, dtype=i32), n_used - 1)
    block_exp = jnp.minimum(jnp.sum((pends[None, :] <= (blk * BLOCK_ROWS)[:, None]).astype(i32), axis=1),
                            N_EXPERTS - 1)
    seg_end = pends[block_exp] // BLOCK_ROWS
    next_exp = jnp.where(seg_end < n_used, block_exp[jnp.minimum(seg_end, n_blocks - 1)], -1).astype(i32)
    dest = _dest_rows(eidx, rel, pstarts[:, None].astype(f32))

    x_sorted = _dispatch(dest, h1, n_blocks * BLOCK_ROWS)
    y_sorted = _expert_ffn(block_exp, n_used[None], next_exp, x_sorted, w_gate_e, w_up_e, w_down_e)
    out = _combine(dest, gate.T, h1, w_gate_s.astype(bf16), w_up_s.astype(bf16), w_down_s.astype(bf16),
                   vec(ln2_g), vec(ln2_b), y_sorted)
    return out.reshape(bsz, seq, d)


def kernel(x, w_in, conv_w, conv_b, dt_bias_f, dt_bias_b, a_log_f, a_log_b, d_skip, ssd_norm_w, lambda_q1, lambda_k1, lambda_q2, lambda_k2, attn_norm_w, w_out, ln1_g, ln1_b, w_router, router_bias, w_gate_e, w_up_e, w_down_e, w_gate_s, w_up_s, w_down_s, ln2_g, ln2_b):
    h = x
    for l in range(DEPTH):
        lambda_init = 0.8 - 0.6 * math.exp(-0.3 * l)
        h = _layer(h, w_in[l], conv_w[l], conv_b[l], dt_bias_f[l], dt_bias_b[l], a_log_f[l], a_log_b[l],
                   d_skip[l], ssd_norm_w[l], lambda_q1[l], lambda_k1[l], lambda_q2[l], lambda_k2[l],
                   attn_norm_w[l], w_out[l], ln1_g[l], ln1_b[l], w_router[l], router_bias[l],
                   w_gate_e[l], w_up_e[l], w_down_e[l], w_gate_s[l], w_up_s[l], w_down_s[l],
                   ln2_g[l], ln2_b[l], lambda_init)
    return h
```

```python
import functools
import math

import jax
import jax.numpy as jnp
from jax import lax
from jax.experimental import pallas as pl
from jax.experimental.pallas import tpu as pltpu

f32 = jnp.float32
bf16 = jnp.bfloat16
i32 = jnp.int32

D_MODEL = 2048
D_SSD = 1024
D_ATTN = 1024
SSD_HEAD_DIM = 64
N_SSD_HEADS = 16
SSD_GROUPS = 2
HEADS_PER_GROUP = N_SSD_HEADS // SSD_GROUPS
D_STATE = 128
CONV_W = 5
CONV_CH = D_SSD + 2 * SSD_GROUPS * D_STATE
CHUNK = 256
ATTN_HEAD_DIM = 64
N_ATTN_HEADS = 8
N_EXPERTS = 64
N_EXPERT_GROUPS = 8
EXPERTS_PER_GROUP = N_EXPERTS // N_EXPERT_GROUPS
TOPK_GROUPS = 4
TOP_K = 8
D_EXPERT = 512
ROUTED_SCALE = 2.5
BLOCK_ROWS = 512
LN_EPS = 1e-5
RMS_EPS = 1e-5
DEPTH = 1
ALPHA = (2 * DEPTH) ** 0.25

LANES = 128
BF16_SUBLANES = 16
VMEM_LIMIT = 56 * 1024 * 1024

PROJ_MAIN = D_SSD + CONV_CH + 3 * D_ATTN
COL_XBC = D_SSD
COL_Q = D_SSD + CONV_CH
COL_K = COL_Q + D_ATTN
COL_V = COL_K + D_ATTN

def _sigmoid(x):
    return 1.0 / (1.0 + jnp.exp(-x))


def _silu(x):
    return x * _sigmoid(x)


def _softplus(x):
    return jnp.maximum(x, 0.0) + jnp.log(1.0 + jnp.exp(-jnp.abs(x)))


def _layer_norm(r, g, b):
    mu = jnp.mean(r, axis=-1, keepdims=True)
    c = r - mu
    var = jnp.mean(c * c, axis=-1, keepdims=True)
    return c * lax.rsqrt(var + LN_EPS) * g + b


def _params(*sem):
    return pltpu.CompilerParams(dimension_semantics=sem, vmem_limit_bytes=VMEM_LIMIT)


def _inproj_kernel(x_ref, w_ref, wdt_ref, o_ref, dt_ref, xb_ref):
    @pl.when(pl.program_id(1) == 0)
    def _():
        xb_ref[...] = x_ref[...].astype(bf16)
        dt_ref[...] = jnp.dot(xb_ref[...], wdt_ref[...], preferred_element_type=f32)

    o_ref[...] = jnp.dot(xb_ref[...], w_ref[...], preferred_element_type=f32).astype(bf16)


def _in_projection(x2, w_main, w_dt):
    n_tok = x2.shape[0]
    tm = min(1024, n_tok)
    tn = PROJ_MAIN // 4
    return pl.pallas_call(
        _inproj_kernel,
        grid=(n_tok // tm, PROJ_MAIN // tn),
        in_specs=[
            pl.BlockSpec((tm, D_MODEL), lambda i, j: (i, 0)),
            pl.BlockSpec((D_MODEL, tn), lambda i, j: (0, j)),
            pl.BlockSpec((D_MODEL, LANES), lambda i, j: (0, 0)),
        ],
        out_specs=[
            pl.BlockSpec((tm, tn), lambda i, j: (i, j)),
            pl.BlockSpec((tm, LANES), lambda i, j: (i, 0)),
        ],
        out_shape=[
            jax.ShapeDtypeStruct((n_tok, PROJ_MAIN), bf16),
            jax.ShapeDtypeStruct((n_tok, LANES), f32),
        ],
        scratch_shapes=[pltpu.VMEM((tm, D_MODEL), bf16)],
        compiler_params=_params("parallel", "arbitrary"),
        name="in_projection",
    )(x2, w_main, w_dt)


CONV_HALO = 16


def _conv_kernel(x_ref, w_ref, b_ref, o_ref, pad_ref):
    seq = x_ref.shape[1]
    tc = x_ref.shape[2]
    zeros = jnp.zeros((CONV_HALO, tc), f32)
    pad_ref[0:CONV_HALO, :] = zeros
    pad_ref[seq + CONV_HALO:seq + 2 * CONV_HALO, :] = zeros
    w = w_ref[...]
    b = b_ref[...]
    win = CHUNK + 2 * CONV_HALO

    def fill(i, carry):
        base = pl.multiple_of(i * CHUNK, CHUNK)
        pad_ref[pl.ds(base + CONV_HALO, CHUNK), :] = x_ref[0, pl.ds(base, CHUNK), :].astype(f32)
        return carry

    lax.fori_loop(0, seq // CHUNK, fill, 0)

    def body(i, carry):
        base = pl.multiple_of(i * CHUNK, CHUNK)
        xe = pad_ref[pl.ds(base, win), :]
        acc = jnp.zeros((CHUNK, tc), f32) + b
        for k in range(CONV_W):
            d = k - CONV_W // 2
            r = xe if d == 0 else pltpu.roll(xe, (-d) % win, 0)
            acc = acc + r[CONV_HALO:CONV_HALO + CHUNK, :] * w[k:k + 1, :]
        o_ref[0, pl.ds(base, CHUNK), :] = _silu(acc).astype(bf16)
        return carry

    lax.fori_loop(0, seq // CHUNK, body, 0)


def _conv_silu(proj3, conv_w8, conv_b2):
    bsz, seq, _ = proj3.shape
    tc = 256
    col0 = COL_XBC // tc
    return pl.pallas_call(
        _conv_kernel,
        grid=(bsz, CONV_CH // tc),
        in_specs=[
            pl.BlockSpec((1, seq, tc), lambda b, j: (b, 0, col0 + j)),
            pl.BlockSpec((8, tc), lambda b, j: (0, j)),
            pl.BlockSpec((1, tc), lambda b, j: (0, j)),
        ],
        out_specs=pl.BlockSpec((1, seq, tc), lambda b, j: (b, 0, j)),
        out_shape=jax.ShapeDtypeStruct((bsz, seq, CONV_CH), bf16),
        scratch_shapes=[pltpu.VMEM((seq + 2 * CONV_HALO, tc), f32)],
        compiler_params=_params("parallel", "parallel"),
        name="conv_silu",
    )(proj3, conv_w8, conv_b2)


def _ssd_cumsums(dt_ref, bias_ref, arow_ref):
    dtv = _softplus(dt_ref[0] + bias_ref[...])
    adt = dtv * arow_ref[...]
    li = lax.broadcasted_iota(i32, (CHUNK, CHUNK), 0)
    si = lax.broadcasted_iota(i32, (CHUNK, CHUNK), 1)
    tril = (si <= li).astype(bf16)
    triu = (si >= li).astype(bf16)
    lane = lax.broadcasted_iota(i32, (CHUNK, LANES), 1)
    hi = adt.astype(bf16)
    r1 = adt - hi.astype(f32)
    mid = r1.astype(bf16)
    lo = (r1 - mid.astype(f32)).astype(bf16)
    parts = jnp.concatenate([hi, mid, lo], axis=1)

    def tri_sum(tri):
        t = jnp.dot(tri, parts, preferred_element_type=f32)
        return t[:, :LANES] + t[:, LANES:2 * LANES] + t[:, 2 * LANES:]

    cs = jnp.where(lane < N_SSD_HEADS, tri_sum(tril), tri_sum(triu))
    return dtv, cs


def _expand_heads(v, ex_ref):
    hi = v.astype(bf16)
    lo = (v - hi.astype(f32)).astype(bf16)
    return jnp.dot(jnp.concatenate([hi, lo], axis=1), ex_ref[...], preferred_element_type=f32)


def _state_update(h_ref, bc, xw, decay_row, b_col0):
    gw = HEADS_PER_GROUP * SSD_HEAD_DIM
    new = []
    for g in range(SSD_GROUPS):
        bg = bc[:, b_col0 + g * D_STATE:b_col0 + (g + 1) * D_STATE]
        bgt = bg.astype(f32).T.astype(bf16)
        new.append(jnp.dot(bgt, xw[:, g * gw:(g + 1) * gw], preferred_element_type=f32))
    h_ref[...] = h_ref[...] * decay_row + jnp.concatenate(new, axis=1)


def _state_readout(h_ref, bc, c_col0):
    gw = HEADS_PER_GROUP * SSD_HEAD_DIM
    outs = []
    for g in range(SSD_GROUPS):
        cg = bc[:, c_col0 + g * D_STATE:c_col0 + (g + 1) * D_STATE]
        outs.append(jnp.dot(cg, h_ref[:, g * gw:(g + 1) * gw].astype(bf16), preferred_element_type=f32))
    return jnp.concatenate(outs, axis=1)


C_COL0 = SSD_GROUPS * D_STATE


def _ssd_fwd_kernel(x_ref, bc_ref, dt_ref, bias_ref, arow_ref, dskip_ref, exf_ref, y_ref, h_ref):
    @pl.when(pl.program_id(1) == 0)
    def _():
        h_ref[...] = jnp.zeros_like(h_ref)

    x = x_ref[0]
    bc = bc_ref[0]
    dtv, cs = _ssd_cumsums(dt_ref, bias_ref, arow_ref)
    cst = cs.T
    dtt = dtv.T
    li = lax.broadcasted_iota(i32, (CHUNK, CHUNK), 0)
    si = lax.broadcasted_iota(i32, (CHUNK, CHUNK), 1)
    lower = li > si
    upper = li < si
    lower_eq = li >= si
    lane = lax.broadcasted_iota(i32, (CHUNK, LANES), 1)
    first_half = lane < SSD_HEAD_DIM

    cb = []
    for g in range(SSD_GROUPS):
        bg = bc[:, g * D_STATE:(g + 1) * D_STATE]
        cg = bc[:, C_COL0 + g * D_STATE:C_COL0 + (g + 1) * D_STATE]
        cb.append(lax.dot_general(cg, bg, (((1,), (1,)), ((), ())), preferred_element_type=f32))

    pairs = []
    for j in range(N_SSD_HEADS // 2):
        xpair = x[:, j * LANES:(j + 1) * LANES]
        acc = None
        for h, xm in ((2 * j, jnp.where(first_half, xpair, jnp.zeros_like(xpair))),
                      (2 * j + 1, jnp.where(first_half, jnp.zeros_like(xpair), xpair))):
            hb = N_SSD_HEADS + h
            arg = jnp.where(lower_eq, cs[:, h:h + 1] - cst[h:h + 1, :], cs[:, hb:hb + 1] - cst[hb:hb + 1, :])
            dtf = dtt[h:h + 1, :]
            dtb = dtt[hb:hb + 1, :]
            wgt = jnp.where(lower, dtf, jnp.where(upper, dtb, dtf + dtb))
            m = (cb[h // HEADS_PER_GROUP] * jnp.exp(arg) * wgt).astype(bf16)
            t = jnp.dot(m, xm, preferred_element_type=f32)
            acc = t if acc is None else acc + t
        pairs.append(acc)
    y = jnp.concatenate(pairs, axis=1)

    xf = x.astype(f32)
    e_f = _expand_heads(jnp.exp(cs), exf_ref)
    y = y + _state_readout(h_ref, bc, C_COL0) * e_f + dskip_ref[...] * xf
    y_ref[0] = y

    to_end = jnp.where(lane < N_SSD_HEADS, cs[CHUNK - 1:CHUNK, :] - cs, 0.0)
    w_f = _expand_heads(jnp.exp(to_end) * dtv, exf_ref)
    _state_update(h_ref, bc, (xf * w_f).astype(bf16), e_f[CHUNK - 1:CHUNK, :], 0)


def _ssd_bwd_kernel(x_ref, bc_ref, dt_ref, z_ref, yp_ref, bias_ref, arow_ref, exb_ref, nw_ref, o_ref, h_ref):
    @pl.when(pl.program_id(1) == 0)
    def _():
        h_ref[...] = jnp.zeros_like(h_ref)

    x = x_ref[0]
    bc = bc_ref[0]
    dtv, cs = _ssd_cumsums(dt_ref, bias_ref, arow_ref)
    e_b = _expand_heads(jnp.exp(cs), exb_ref)
    y = yp_ref[0] + _state_readout(h_ref, bc, C_COL0) * e_b

    lane = lax.broadcasted_iota(i32, (CHUNK, LANES), 1)
    to_start = jnp.where((lane >= N_SSD_HEADS) & (lane < 2 * N_SSD_HEADS), cs[0:1, :] - cs, 0.0)
    w_b = _expand_heads(jnp.exp(to_start) * dtv, exb_ref)
    _state_update(h_ref, bc, (x.astype(f32) * w_b).astype(bf16), e_b[0:1, :], 0)

    y = y * _silu(z_ref[0].astype(f32))
    gw = D_SSD // SSD_GROUPS
    outs = []
    for g in range(SSD_GROUPS):
        yg = y[:, g * gw:(g + 1) * gw]
        outs.append(yg * lax.rsqrt(jnp.mean(yg * yg, axis=-1, keepdims=True) + RMS_EPS))
    o_ref[0] = (jnp.concatenate(outs, axis=1) * nw_ref[...]).astype(bf16)


def _ssd(proj3, xconv, dt3, bias_row, a_row, dskip_row, ex_f, ex_b, norm_w_row):
    bsz, seq, _ = proj3.shape
    nc = seq // CHUNK
    bc_blk = D_SSD // (2 * SSD_GROUPS * D_STATE)
    row = lambda n: pl.BlockSpec((1, n), lambda b, c: (0, 0))
    ex_spec = pl.BlockSpec((2 * LANES, D_SSD), lambda b, c: (0, 0))
    fwd = lambda b, c: (b, c, 0)
    y_part = pl.pallas_call(
        _ssd_fwd_kernel,
        grid=(bsz, nc),
        in_specs=[
            pl.BlockSpec((1, CHUNK, D_SSD), fwd),
            pl.BlockSpec((1, CHUNK, 2 * SSD_GROUPS * D_STATE), lambda b, c: (b, c, bc_blk)),
            pl.BlockSpec((1, CHUNK, LANES), fwd),
            row(LANES), row(LANES), row(D_SSD), ex_spec,
        ],
        out_specs=pl.BlockSpec((1, CHUNK, D_SSD), fwd),
        out_shape=jax.ShapeDtypeStruct((bsz, seq, D_SSD), f32),
        scratch_shapes=[pltpu.VMEM((D_STATE, D_SSD), f32)],
        compiler_params=_params("parallel", "arbitrary"),
        name="ssd_forward_sweep",
    )(xconv, xconv, dt3, bias_row, a_row, dskip_row, ex_f)

    rev = lambda b, c: (b, nc - 1 - c, 0)
    return pl.pallas_call(
        _ssd_bwd_kernel,
        grid=(bsz, nc),
        in_specs=[
            pl.BlockSpec((1, CHUNK, D_SSD), rev),
            pl.BlockSpec((1, CHUNK, 2 * SSD_GROUPS * D_STATE), lambda b, c: (b, nc - 1 - c, bc_blk)),
            pl.BlockSpec((1, CHUNK, LANES), rev),
            pl.BlockSpec((1, CHUNK, D_SSD), rev),
            pl.BlockSpec((1, CHUNK, D_SSD), rev),
            row(LANES), row(LANES), ex_spec, row(D_SSD),
        ],
        out_specs=pl.BlockSpec((1, CHUNK, D_SSD), rev),
        out_shape=jax.ShapeDtypeStruct((bsz, seq, D_SSD), bf16),
        scratch_shapes=[pltpu.VMEM((D_STATE, D_SSD), f32)],
        compiler_params=_params("parallel", "arbitrary"),
        name="ssd_backward_sweep",
    )(xconv, xconv, dt3, proj3, y_part, bias_row, a_row, ex_b, norm_w_row)


ATTN_TQ = 256


def _attn_kernel(q_ref, k_ref, v_ref, slope_ref, lq1_ref, lk1_ref, lq2_ref, lk2_ref, nw_ref, o_ref, *, lambda_init):
    q = q_ref[0]
    k = k_ref[0]
    v = v_ref[0]
    tq = q.shape[0]
    seq = k.shape[0]
    lam = (jnp.exp(jnp.sum(lq1_ref[...] * lk1_ref[...], axis=-1, keepdims=True))
           - jnp.exp(jnp.sum(lq2_ref[...] * lk2_ref[...], axis=-1, keepdims=True)) + lambda_init)
    slope = slope_ref[0][:, 0:1]
    qpos = (pl.program_id(2) * tq + lax.broadcasted_iota(i32, (tq, seq), 0)).astype(f32)
    kpos = lax.broadcasted_iota(i32, (tq, seq), 1).astype(f32)
    bias = -slope * jnp.abs(qpos - kpos)
    lane = lax.broadcasted_iota(i32, q.shape, 1)
    first_half = lane < ATTN_HEAD_DIM
    zero = jnp.zeros_like(q)
    scale = ATTN_HEAD_DIM ** -0.5

    def softmax_map(qm):
        s = lax.dot_general(qm, k, (((1,), (1,)), ((), ())), preferred_element_type=f32) * scale + bias
        p = jnp.exp(s - jnp.max(s, axis=-1, keepdims=True))
        return p / jnp.sum(p, axis=-1, keepdims=True)

    att = softmax_map(jnp.where(first_half, q, zero)) - lam * softmax_map(jnp.where(first_half, zero, q))
    o = jnp.dot(att.astype(bf16), v, preferred_element_type=f32)
    o = o * lax.rsqrt(jnp.mean(o * o, axis=-1, keepdims=True) + RMS_EPS) * nw_ref[...] * (1.0 - lambda_init)
    o_ref[0] = o.astype(bf16)


def _diff_attention(proj3, slopes3, lq1, lk1, lq2, lk2, norm_w_row, lambda_init):
    bsz, seq, _ = proj3.shape
    tq = min(ATTN_TQ, seq)
    hd = 2 * ATTN_HEAD_DIM
    qb, kb, vb = COL_Q // hd, COL_K // hd, COL_V // hd
    vec = lambda n: pl.BlockSpec((1, n), lambda b, h, i: (0, 0))
    return pl.pallas_call(
        functools.partial(_attn_kernel, lambda_init=lambda_init),
        grid=(bsz, N_ATTN_HEADS, seq // tq),
        in_specs=[
            pl.BlockSpec((1, tq, hd), lambda b, h, i: (b, i, qb + h)),
            pl.BlockSpec((1, seq, hd), lambda b, h, i: (b, 0, kb + h)),
            pl.BlockSpec((1, seq, hd), lambda b, h, i: (b, 0, vb + h)),
            pl.BlockSpec((1, 1, LANES), lambda b, h, i: (h, 0, 0)),
            vec(ATTN_HEAD_DIM), vec(ATTN_HEAD_DIM), vec(ATTN_HEAD_DIM), vec(ATTN_HEAD_DIM), vec(hd),
        ],
        out_specs=pl.BlockSpec((1, tq, hd), lambda b, h, i: (b, i, h)),
        out_shape=jax.ShapeDtypeStruct((bsz, seq, D_ATTN), bf16),
        compiler_params=_params("parallel", "parallel", "arbitrary"),
        name="diff_attention",
    )(proj3, proj3, proj3, slopes3, lq1, lk1, lq2, lk2, norm_w_row)


def _outproj_kernel(ys_ref, ya_ref, x_ref, wa_ref, wb_ref, g_ref, b_ref, wr1_ref, wr2_ref, h_ref, lg_ref):
    mix = (jnp.dot(ys_ref[...], wa_ref[...], preferred_element_type=f32)
           + jnp.dot(ya_ref[...], wb_ref[...], preferred_element_type=f32))
    h = _layer_norm(ALPHA * x_ref[...] + mix, g_ref[...], b_ref[...])
    h_ref[...] = h
    h_hi = h.astype(bf16)
    h_lo = (h - h_hi.astype(f32)).astype(bf16)
    lg = (jnp.dot(h_hi, wr1_ref[...], preferred_element_type=f32)
          + jnp.dot(h_lo, wr2_ref[...], preferred_element_type=f32))
    lgt = lg.T
    lg_ref[...] = lgt[:N_EXPERTS] + lgt[N_EXPERTS:]


def _out_projection(y_ssd2, y_attn2, x2, w_a, w_b, g_row, b_row, w_router):
    n_tok = x2.shape[0]
    tm = min(512, n_tok)
    const = lambda shape: pl.BlockSpec(shape, lambda i: (0, 0), pipeline_mode=pl.Buffered(1))
    w_hi = w_router.astype(bf16)
    w_lo = (w_router - w_hi.astype(f32)).astype(bf16)
    wr1 = jnp.concatenate([w_hi, w_lo], axis=1)
    wr2 = jnp.concatenate([w_hi, jnp.zeros_like(w_hi)], axis=1)
    return pl.pallas_call(
        _outproj_kernel,
        grid=(n_tok // tm,),
        in_specs=[
            pl.BlockSpec((tm, D_SSD), lambda i: (i, 0)),
            pl.BlockSpec((tm, D_ATTN), lambda i: (i, 0)),
            pl.BlockSpec((tm, D_MODEL), lambda i: (i, 0)),
            const((D_SSD, D_MODEL)), const((D_ATTN, D_MODEL)),
            const((1, D_MODEL)), const((1, D_MODEL)),
            const((D_MODEL, 2 * N_EXPERTS)), const((D_MODEL, 2 * N_EXPERTS)),
        ],
        out_specs=[
            pl.BlockSpec((tm, D_MODEL), lambda i: (i, 0)),
            pl.BlockSpec((N_EXPERTS, tm), lambda i: (0, i)),
        ],
        out_shape=[
            jax.ShapeDtypeStruct((n_tok, D_MODEL), f32),
            jax.ShapeDtypeStruct((N_EXPERTS, n_tok), f32),
        ],
        compiler_params=_params("parallel"),
        name="out_projection_ln",
    )(y_ssd2, y_attn2, x2, w_a, w_b, g_row, b_row, wr1, wr2)


ROUTE_TT = 512


def _route_kernel(lg_ref, bias_ref, eidx_ref, gate_ref, rel_ref, cnt_ref, run_ref):
    @pl.when(pl.program_id(0) == 0)
    def _():
        run_ref[...] = jnp.zeros_like(run_ref)

    tt = lg_ref.shape[2]
    shape3 = (N_EXPERT_GROUPS, EXPERTS_PER_GROUP, tt)
    neg = -jnp.inf
    sc = _sigmoid(lg_ref[...])
    bi = sc + bias_ref[...]
    gidx = lax.broadcasted_iota(i32, shape3, 0).astype(f32)
    jidx = lax.broadcasted_iota(i32, shape3, 1).astype(f32)
    eid = gidx * EXPERTS_PER_GROUP + jidx

    m1 = jnp.max(bi, axis=1, keepdims=True)
    i1 = jnp.min(jnp.where(bi == m1, jidx, float(EXPERTS_PER_GROUP)), axis=1, keepdims=True)
    m2 = jnp.max(jnp.where(jidx == i1, neg, bi), axis=1, keepdims=True)
    gs = m1 + m2
    gcol = lax.broadcasted_iota(i32, gs.shape, 0).astype(f32)
    gmask = jnp.zeros(gs.shape, f32)
    for _ in range(TOPK_GROUPS):
        m = jnp.max(gs, axis=0, keepdims=True)
        ig = jnp.min(jnp.where(gs == m, gcol, float(N_EXPERT_GROUPS)), axis=0, keepdims=True)
        pick = gcol == ig
        gmask = jnp.where(pick, 1.0, gmask)
        gs = jnp.where(pick, neg, gs)

    val = jnp.where(gmask > 0.0, bi, neg)
    sel = jnp.zeros(shape3, f32)
    gates = []
    for k in range(TOP_K):
        m = jnp.max(jnp.max(val, axis=1, keepdims=True), axis=0, keepdims=True)
        ie = jnp.min(jnp.min(jnp.where(val == m, eid, float(N_EXPERTS)), axis=1, keepdims=True), axis=0, keepdims=True)
        pick = eid == ie
        gates.append(jnp.sum(jnp.sum(jnp.where(pick, sc, 0.0), axis=1, keepdims=True), axis=0, keepdims=True)[0])
        eidx_ref[k:k + 1, :] = ie[0].astype(i32)
        sel = jnp.where(pick, 1.0, sel)
        val = jnp.where(pick, neg, val)
    gsum = gates[0]
    for k in range(1, TOP_K):
        gsum = gsum + gates[k]
    for k in range(TOP_K):
        gate_ref[k:k + 1, :] = gates[k] / gsum * ROUTED_SCALE

    sel2 = sel.reshape(N_EXPERTS, tt)
    s_i = lax.broadcasted_iota(i32, (tt, tt), 0)
    t_i = lax.broadcasted_iota(i32, (tt, tt), 1)
    before = (s_i < t_i).astype(bf16)
    pos = jnp.dot(sel2.astype(bf16), before, preferred_element_type=f32) + run_ref[...]
    erow = lax.broadcasted_iota(i32, (N_EXPERTS, tt), 0)
    for k in range(TOP_K):
        pick2 = erow == eidx_ref[k:k + 1, :]
        rel_ref[k:k + 1, :] = jnp.sum(jnp.where(pick2, pos, 0.0), axis=0, keepdims=True).astype(i32)
    run_ref[...] = run_ref[...] + jnp.sum(sel2, axis=1, keepdims=True)
    cnt_ref[...] = jnp.broadcast_to(run_ref[...], cnt_ref.shape)


def _route(logits3, bias3):
    n_tok = logits3.shape[2]
    tt = min(ROUTE_TT, n_tok)
    tok_blk = pl.BlockSpec((TOP_K, tt), lambda i: (0, i))
    return pl.pallas_call(
        _route_kernel,
        grid=(n_tok // tt,),
        in_specs=[
            pl.BlockSpec((N_EXPERT_GROUPS, EXPERTS_PER_GROUP, tt), lambda i: (0, 0, i)),
            pl.BlockSpec((N_EXPERT_GROUPS, EXPERTS_PER_GROUP, 1), lambda i: (0, 0, 0)),
        ],
        out_specs=[tok_blk, tok_blk, tok_blk, pl.BlockSpec((N_EXPERTS, LANES), lambda i: (0, 0))],
        out_shape=[
            jax.ShapeDtypeStruct((TOP_K, n_tok), i32),
            jax.ShapeDtypeStruct((TOP_K, n_tok), f32),
            jax.ShapeDtypeStruct((TOP_K, n_tok), i32),
            jax.ShapeDtypeStruct((N_EXPERTS, LANES), f32),
        ],
        scratch_shapes=[pltpu.VMEM((N_EXPERTS, 1), f32)],
        compiler_params=_params("arbitrary"),
        name="route_topk",
    )(logits3, bias3)


def _dest_kernel(eidx_ref, rel_ref, pstart_ref, dest_ref):
    tt = eidx_ref.shape[1]
    erow = lax.broadcasted_iota(i32, (N_EXPERTS, tt), 0)
    pstart = pstart_ref[...]
    for k in range(TOP_K):
        pick = erow == eidx_ref[k:k + 1, :]
        base = jnp.sum(jnp.where(pick, pstart, 0.0), axis=0, keepdims=True)
        dest_ref[k:k + 1, :] = rel_ref[k:k + 1, :] + base.astype(i32)


def _dest_rows(eidx, rel, pstart_col):
    n_tok = eidx.shape[1]
    tt = min(ROUTE_TT, n_tok)
    tok_blk = pl.BlockSpec((TOP_K, tt), lambda i: (0, i))
    return pl.pallas_call(
        _dest_kernel,
        grid=(n_tok // tt,),
        in_specs=[tok_blk, tok_blk, pl.BlockSpec((N_EXPERTS, 1), lambda i: (0, 0))],
        out_specs=tok_blk,
        out_shape=jax.ShapeDtypeStruct((TOP_K, n_tok), i32),
        compiler_params=_params("parallel"),
        name="dest_rows",
    )(eidx, rel, pstart_col)


MOE_TM = 256
DISPATCH_TM = 512


def _tile_slots(dest, tm):
    return dest.reshape(TOP_K, dest.shape[1] // tm, tm).transpose(1, 0, 2).reshape(-1)


def _dispatch_kernel(dest_ref, h_ref, xs_ref, sem):
    tm = h_ref.shape[0]

    def issue(t, carry):
        for k in range(TOP_K):
            pltpu.make_async_copy(h_ref.at[pl.ds(t, 1)], xs_ref.at[pl.ds(dest_ref[k * tm + t], 1)], sem).start()
        return carry

    lax.fori_loop(0, tm, issue, 0)
    for _ in range(TOP_K):
        pltpu.make_async_copy(h_ref, xs_ref.at[pl.ds(0, tm)], sem).wait()


def _dispatch(dest, h1, n_rows):
    n_tok = h1.shape[0]
    tm = min(DISPATCH_TM, n_tok)
    return pl.pallas_call(
        _dispatch_kernel,
        grid=(n_tok // tm,),
        in_specs=[
            pl.BlockSpec((TOP_K * tm,), lambda i: (i,), memory_space=pltpu.SMEM),
            pl.BlockSpec((tm, D_MODEL), lambda i: (i, 0)),
        ],
        out_specs=pl.BlockSpec(memory_space=pl.ANY),
        out_shape=jax.ShapeDtypeStruct((n_rows, D_MODEL), f32),
        scratch_shapes=[pltpu.SemaphoreType.DMA(())],
        compiler_params=_params("arbitrary"),
        name="moe_dispatch",
    )(_tile_slots(dest, tm), h1)


def _expert_kernel(bexp_ref, nused_ref, next_ref, x_ref, wg_hbm, wu_hbm, wd_hbm, o_ref,
                   stage_g, stage_u, stage_d, wg_s, wu_s, wd_s, sem):
    i = pl.program_id(0)
    e = bexp_ref[i]
    first_block_of_expert = (i == 0) | (e != bexp_ref[jnp.maximum(i - 1, 0)])

    def weight_copies(expert):
        return (pltpu.make_async_copy(wg_hbm.at[expert], stage_g, sem.at[0]),
                pltpu.make_async_copy(wu_hbm.at[expert], stage_u, sem.at[1]),
                pltpu.make_async_copy(wd_hbm.at[expert], stage_d, sem.at[2]))

    @pl.when(i == 0)
    def _():
        for c in weight_copies(e):
            c.start()

    @pl.when(first_block_of_expert)
    def _():
        for c in weight_copies(e):
            c.wait()
        wg_s[...] = stage_g[...].astype(bf16)
        wu_s[...] = stage_u[...].astype(bf16)
        wd_s[...] = stage_d[...].astype(bf16)

        @pl.when(next_ref[i] >= 0)
        def _():
            for c in weight_copies(next_ref[i]):
                c.start()

    @pl.when(i < nused_ref[0])
    def _():
        xb = x_ref[...].astype(bf16)
        hg = jnp.dot(xb, wg_s[...], preferred_element_type=f32)
        hu = jnp.dot(xb, wu_s[...], preferred_element_type=f32)
        hb = (_silu(hg) * hu).astype(bf16)
        o_ref[...] = jnp.dot(hb, wd_s[...], preferred_element_type=f32)


def _expert_ffn(block_exp, n_used, next_exp, x_sorted, w_gate_e, w_up_e, w_down_e):
    n_blocks = x_sorted.shape[0] // BLOCK_ROWS
    rows = lambda i, be, nu, nx: (jnp.minimum(i, nu[0] - 1), 0)
    blk = pl.BlockSpec((BLOCK_ROWS, D_MODEL), rows)
    hbm = pl.BlockSpec(memory_space=pl.ANY)
    return pl.pallas_call(
        _expert_kernel,
        grid_spec=pltpu.PrefetchScalarGridSpec(
            num_scalar_prefetch=3,
            grid=(n_blocks,),
            in_specs=[blk, hbm, hbm, hbm],
            out_specs=blk,
            scratch_shapes=[
                pltpu.VMEM((D_MODEL, D_EXPERT), f32),
                pltpu.VMEM((D_MODEL, D_EXPERT), f32),
                pltpu.VMEM((D_EXPERT, D_MODEL), f32),
                pltpu.VMEM((D_MODEL, D_EXPERT), bf16),
                pltpu.VMEM((D_MODEL, D_EXPERT), bf16),
                pltpu.VMEM((D_EXPERT, D_MODEL), bf16),
                pltpu.SemaphoreType.DMA((3,)),
            ],
        ),
        out_shape=jax.ShapeDtypeStruct(x_sorted.shape, f32),
        compiler_params=_params("arbitrary"),
        name="moe_expert_ffn",
    )(block_exp, n_used, next_exp, x_sorted, w_gate_e, w_up_e, w_down_e)


def _combine_kernel(dest_ref, gate_ref, h_ref, wg_ref, wu_ref, wd_ref, g_ref, b_ref, ys_ref, o_ref, buf, sem):
    tm = h_ref.shape[0]

    def issue(t, carry):
        for k in range(TOP_K):
            pltpu.make_async_copy(ys_ref.at[pl.ds(dest_ref[k * tm + t], 1)], buf.at[k, pl.ds(t, 1)], sem).start()
        return carry

    lax.fori_loop(0, tm, issue, 0)

    h = h_ref[...]
    hb = h.astype(bf16)
    hg = jnp.dot(hb, wg_ref[...], preferred_element_type=f32)
    hu = jnp.dot(hb, wu_ref[...], preferred_element_type=f32)
    ffn = jnp.dot((_silu(hg) * hu).astype(bf16), wd_ref[...], preferred_element_type=f32)

    for k in range(TOP_K):
        pltpu.make_async_copy(ys_ref.at[pl.ds(0, tm)], buf.at[k], sem).wait()
    gate = gate_ref[...]
    for k in range(TOP_K):
        ffn = ffn + gate[:, k:k + 1] * buf[k]
    o_ref[...] = _layer_norm(ALPHA * h + ffn, g_ref[...], b_ref[...])


def _combine(dest, gate_t, h1, w_gate_s, w_up_s, w_down_s, g_row, b_row, y_sorted):
    n_tok = h1.shape[0]
    tm = min(MOE_TM, n_tok)
    const = lambda shape: pl.BlockSpec(shape, lambda i: (0, 0), pipeline_mode=pl.Buffered(1))
    return pl.pallas_call(
        _combine_kernel,
        grid=(n_tok // tm,),
        in_specs=[
            pl.BlockSpec((TOP_K * tm,), lambda i: (i,), memory_space=pltpu.SMEM),
            pl.BlockSpec((tm, TOP_K), lambda i: (i, 0)),
            pl.BlockSpec((tm, D_MODEL), lambda i: (i, 0)),
            const((D_MODEL, D_EXPERT)), const((D_MODEL, D_EXPERT)), const((D_EXPERT, D_MODEL)),
            const((1, D_MODEL)), const((1, D_MODEL)),
            pl.BlockSpec(memory_space=pl.ANY),
        ],
        out_specs=pl.BlockSpec((tm, D_MODEL), lambda i: (i, 0)),
        out_shape=jax.ShapeDtypeStruct((n_tok, D_MODEL), f32),
        scratch_shapes=[pltpu.VMEM((TOP_K, tm, D_MODEL), f32), pltpu.SemaphoreType.DMA(())],
        compiler_params=_params("arbitrary"),
        name="moe_combine_ln",
    )(_tile_slots(dest, tm), gate_t, h1, w_gate_s, w_up_s, w_down_s, g_row, b_row, y_sorted)


def _head_expansion(first_lane):
    r = jnp.arange(LANES)[:, None]
    c = jnp.arange(D_SSD)[None, :] // SSD_HEAD_DIM
    ex = (r == c + first_lane).astype(bf16)
    return jnp.concatenate([ex, ex], axis=0)


def _pad_lanes(v):
    return jnp.pad(v.astype(f32), (0, LANES - v.shape[0]))[None, :]


def _layer(h3, w_in, conv_w, conv_b, dt_bias_f, dt_bias_b, a_log_f, a_log_b, d_skip, ssd_norm_w,
           lambda_q1, lambda_k1, lambda_q2, lambda_k2, attn_norm_w, w_out, ln1_g, ln1_b,
           w_router, router_bias, w_gate_e, w_up_e, w_down_e, w_gate_s, w_up_s, w_down_s,
           ln2_g, ln2_b, lambda_init):
    bsz, seq, d = h3.shape
    n_tok = bsz * seq
    x2 = h3.reshape(n_tok, d)

    n_dt = 2 * N_SSD_HEADS
    c_dt = D_SSD + CONV_CH
    w_main = jnp.concatenate([w_in[:, :c_dt], w_in[:, c_dt + n_dt:]], axis=1).astype(bf16)
    w_dt = jnp.pad(w_in[:, c_dt:c_dt + n_dt], ((0, 0), (0, LANES - n_dt))).astype(bf16)
    proj, dt = _in_projection(x2, w_main, w_dt)
    proj3 = proj.reshape(bsz, seq, PROJ_MAIN)
    dt3 = dt.reshape(bsz, seq, LANES)

    conv_w8 = jnp.pad(conv_w.astype(f32), ((0, 8 - CONV_W), (0, 0)))
    xconv = _conv_silu(proj3, conv_w8, conv_b.astype(f32)[None, :])
    bias_row = _pad_lanes(jnp.concatenate([dt_bias_f, dt_bias_b]))
    a_row = _pad_lanes(jnp.concatenate([-jnp.exp(a_log_f.astype(f32)), -jnp.exp(a_log_b.astype(f32))]))
    dskip_row = jnp.repeat(d_skip.astype(f32), SSD_HEAD_DIM)[None, :]
    y_ssd = _ssd(proj3, xconv, dt3, bias_row, a_row, dskip_row, _head_expansion(0), _head_expansion(N_SSD_HEADS),
                 ssd_norm_w.astype(f32)[None, :])

    slopes = 2.0 ** (-8.0 * jnp.arange(1, N_ATTN_HEADS + 1, dtype=f32) / N_ATTN_HEADS)
    slopes3 = jnp.broadcast_to(slopes[:, None, None], (N_ATTN_HEADS, 1, LANES))
    vec = lambda v: v.astype(f32)[None, :]
    y_attn = _diff_attention(proj3, slopes3, vec(lambda_q1), vec(lambda_k1), vec(lambda_q2), vec(lambda_k2),
                             vec(attn_norm_w), lambda_init)

    w_out_b = w_out.astype(bf16)
    h1, logits_t = _out_projection(
        y_ssd.reshape(n_tok, D_SSD), y_attn.reshape(n_tok, D_ATTN), x2, w_out_b[:D_SSD], w_out_b[D_SSD:],
        vec(ln1_g), vec(ln1_b), w_router.astype(f32))

    eidx, gate, rel, counts = _route(
        logits_t.reshape(N_EXPERT_GROUPS, EXPERTS_PER_GROUP, n_tok),
        router_bias.astype(f32).reshape(N_EXPERT_GROUPS, EXPERTS_PER_GROUP, 1))
    counts = counts[:, 0].astype(i32)
    padded = (counts + BLOCK_ROWS - 1) // BLOCK_ROWS * BLOCK_ROWS
    pends = jnp.cumsum(padded)
    pstarts = pends - padded
    n_blocks = -(-(n_tok * TOP_K) // BLOCK_ROWS) + N_EXPERTS
    n_used = (pends[-1] // BLOCK_ROWS).astype(i32)
    blk = jnp.minimum(jnp.arange(n_blocks, dtype=i32), n_used - 1)
    block_exp = jnp.minimum(jnp.sum((pends[None, :] <= (blk * BLOCK_ROWS)[:, None]).astype(i32), axis=1),
                            N_EXPERTS - 1)
    seg_end = pends[block_exp] // BLOCK_ROWS
    next_exp = jnp.where(seg_end < n_used, block_exp[jnp.minimum(seg_end, n_blocks - 1)], -1).astype(i32)
    dest = _dest_rows(eidx, rel, pstarts[:, None].astype(f32))

    x_sorted = _dispatch(dest, h1, n_blocks * BLOCK_ROWS)
    y_sorted = _expert_ffn(block_exp, n_used[None], next_exp, x_sorted, w_gate_e, w_up_e, w_down_e)
    out = _combine(dest, gate.T, h1, w_gate_s.astype(bf16), w_up_s.astype(bf16), w_down_s.astype(bf16),
                   vec(ln2_g), vec(ln2_b), y_sorted)
    return out.reshape(bsz, seq, d)


def kernel(x, w_in, conv_w, conv_b, dt_bias_f, dt_bias_b, a_log_f, a_log_b, d_skip, ssd_norm_w, lambda_q1, lambda_k1, lambda_q2, lambda_k2, attn_norm_w, w_out, ln1_g, ln1_b, w_router, router_bias, w_gate_e, w_up_e, w_down_e, w_gate_s, w_up_s, w_down_s, ln2_g, ln2_b):
    h = x
    for l in range(DEPTH):
        lambda_init = 0.8 - 0.6 * math.exp(-0.3 * l)
        h = _layer(h, w_in[l], conv_w[l], conv_b[l], dt_bias_f[l], dt_bias_b[l], a_log_f[l], a_log_b[l],
                   d_skip[l], ssd_norm_w[l], lambda_q1[l], lambda_k1[l], lambda_q2[l], lambda_k2[l],
                   attn_norm_w[l], w_out[l], ln1_g[l], ln1_b[l], w_router[l], router_bias[l],
                   w_gate_e[l], w_up_e[l], w_down_e[l], w_gate_s[l], w_up_s[l], w_down_s[l],
                   ln2_g[l], ln2_b[l], lambda_init)
    return h
```

```python
import functools
import math

import jax
import jax.numpy as jnp
from jax import lax
from jax.experimental import pallas as pl
from jax.experimental.pallas import tpu as pltpu

f32 = jnp.float32
bf16 = jnp.bfloat16
i32 = jnp.int32

D_MODEL = 2048
D_SSD = 1024
D_ATTN = 1024
SSD_HEAD_DIM = 64
N_SSD_HEADS = 16
SSD_GROUPS = 2
HEADS_PER_GROUP = N_SSD_HEADS // SSD_GROUPS
D_STATE = 128
CONV_W = 5
CONV_CH = D_SSD + 2 * SSD_GROUPS * D_STATE
CHUNK = 256
ATTN_HEAD_DIM = 64
N_ATTN_HEADS = 8
N_EXPERTS = 64
N_EXPERT_GROUPS = 8
EXPERTS_PER_GROUP = N_EXPERTS // N_EXPERT_GROUPS
TOPK_GROUPS = 4
TOP_K = 8
D_EXPERT = 512
ROUTED_SCALE = 2.5
BLOCK_ROWS = 512
LN_EPS = 1e-5
RMS_EPS = 1e-5
DEPTH = 1
ALPHA = (2 * DEPTH) ** 0.25

LANES = 128
BF16_SUBLANES = 16
VMEM_LIMIT = 56 * 1024 * 1024

PROJ_MAIN = D_SSD + CONV_CH + 3 * D_ATTN
COL_XBC = D_SSD
COL_Q = D_SSD + CONV_CH
COL_K = COL_Q + D_ATTN
COL_V = COL_K + D_ATTN

def _sigmoid(x):
    return 1.0 / (1.0 + jnp.exp(-x))


def _silu(x):
    return x * _sigmoid(x)


def _softplus(x):
    return jnp.maximum(x, 0.0) + jnp.log(1.0 + jnp.exp(-jnp.abs(x)))


def _layer_norm(r, g, b):
    mu = jnp.mean(r, axis=-1, keepdims=True)
    c = r - mu
    var = jnp.mean(c * c, axis=-1, keepdims=True)
    return c * lax.rsqrt(var + LN_EPS) * g + b


def _params(*sem):
    return pltpu.CompilerParams(dimension_semantics=sem, vmem_limit_bytes=VMEM_LIMIT)


def _inproj_kernel(x_ref, w_ref, wdt_ref, o_ref, dt_ref, xb_ref):
    @pl.when(pl.program_id(1) == 0)
    def _():
        xb_ref[...] = x_ref[...].astype(bf16)
        dt_ref[...] = jnp.dot(xb_ref[...], wdt_ref[...], preferred_element_type=f32)

    o_ref[...] = jnp.dot(xb_ref[...], w_ref[...], preferred_element_type=f32).astype(bf16)


def _in_projection(x2, w_main, w_dt):
    n_tok = x2.shape[0]
    tm = min(1024, n_tok)
    tn = PROJ_MAIN // 4
    return pl.pallas_call(
        _inproj_kernel,
        grid=(n_tok // tm, PROJ_MAIN // tn),
        in_specs=[
            pl.BlockSpec((tm, D_MODEL), lambda i, j: (i, 0)),
            pl.BlockSpec((D_MODEL, tn), lambda i, j: (0, j)),
            pl.BlockSpec((D_MODEL, LANES), lambda i, j: (0, 0)),
        ],
        out_specs=[
            pl.BlockSpec((tm, tn), lambda i, j: (i, j)),
            pl.BlockSpec((tm, LANES), lambda i, j: (i, 0)),
        ],
        out_shape=[
            jax.ShapeDtypeStruct((n_tok, PROJ_MAIN), bf16),
            jax.ShapeDtypeStruct((n_tok, LANES), f32),
        ],
        scratch_shapes=[pltpu.VMEM((tm, D_MODEL), bf16)],
        compiler_params=_params("parallel", "arbitrary"),
        name="in_projection",
    )(x2, w_main, w_dt)


CONV_HALO = 16


def _conv_kernel(x_ref, w_ref, b_ref, o_ref, pad_ref):
    seq = x_ref.shape[1]
    tc = x_ref.shape[2]
    zeros = jnp.zeros((CONV_HALO, tc), f32)
    pad_ref[0:CONV_HALO, :] = zeros
    pad_ref[seq + CONV_HALO:seq + 2 * CONV_HALO, :] = zeros
    w = w_ref[...]
    b = b_ref[...]
    win = CHUNK + 2 * CONV_HALO

    def fill(i, carry):
        base = pl.multiple_of(i * CHUNK, CHUNK)
        pad_ref[pl.ds(base + CONV_HALO, CHUNK), :] = x_ref[0, pl.ds(base, CHUNK), :].astype(f32)
        return carry

    lax.fori_loop(0, seq // CHUNK, fill, 0)

    def body(i, carry):
        base = pl.multiple_of(i * CHUNK, CHUNK)
        xe = pad_ref[pl.ds(base, win), :]
        acc = jnp.zeros((CHUNK, tc), f32) + b
        for k in range(CONV_W):
            d = k - CONV_W // 2
            r = xe if d == 0 else pltpu.roll(xe, (-d) % win, 0)
            acc = acc + r[CONV_HALO:CONV_HALO + CHUNK, :] * w[k:k + 1, :]
        o_ref[0, pl.ds(base, CHUNK), :] = _silu(acc).astype(bf16)
        return carry

    lax.fori_loop(0, seq // CHUNK, body, 0)


def _conv_silu(proj3, conv_w8, conv_b2):
    bsz, seq, _ = proj3.shape
    tc = 256
    col0 = COL_XBC // tc
    return pl.pallas_call(
        _conv_kernel,
        grid=(bsz, CONV_CH // tc),
        in_specs=[
            pl.BlockSpec((1, seq, tc), lambda b, j: (b, 0, col0 + j)),
            pl.BlockSpec((8, tc), lambda b, j: (0, j)),
            pl.BlockSpec((1, tc), lambda b, j: (0, j)),
        ],
        out_specs=pl.BlockSpec((1, seq, tc), lambda b, j: (b, 0, j)),
        out_shape=jax.ShapeDtypeStruct((bsz, seq, CONV_CH), bf16),
        scratch_shapes=[pltpu.VMEM((seq + 2 * CONV_HALO, tc), f32)],
        compiler_params=_params("parallel", "parallel"),
        name="conv_silu",
    )(proj3, conv_w8, conv_b2)


def _ssd_cumsums(dt_ref, bias_ref, arow_ref):
    dtv = _softplus(dt_ref[0] + bias_ref[...])
    adt = dtv * arow_ref[...]
    li = lax.broadcasted_iota(i32, (CHUNK, CHUNK), 0)
    si = lax.broadcasted_iota(i32, (CHUNK, CHUNK), 1)
    tril = (si <= li).astype(bf16)
    triu = (si >= li).astype(bf16)
    lane = lax.broadcasted_iota(i32, (CHUNK, LANES), 1)
    hi = adt.astype(bf16)
    r1 = adt - hi.astype(f32)
    mid = r1.astype(bf16)
    lo = (r1 - mid.astype(f32)).astype(bf16)
    parts = jnp.concatenate([hi, mid, lo], axis=1)

    def tri_sum(tri):
        t = jnp.dot(tri, parts, preferred_element_type=f32)
        return t[:, :LANES] + t[:, LANES:2 * LANES] + t[:, 2 * LANES:]

    cs = jnp.where(lane < N_SSD_HEADS, tri_sum(tril), tri_sum(triu))
    return dtv, cs


def _expand_heads(v, ex_ref):
    hi = v.astype(bf16)
    lo = (v - hi.astype(f32)).astype(bf16)
    return jnp.dot(jnp.concatenate([hi, lo], axis=1), ex_ref[...], preferred_element_type=f32)


def _state_update(h_ref, bc, xw, decay_row, b_col0):
    gw = HEADS_PER_GROUP * SSD_HEAD_DIM
    new = []
    for g in range(SSD_GROUPS):
        bg = bc[:, b_col0 + g * D_STATE:b_col0 + (g + 1) * D_STATE]
        bgt = bg.astype(f32).T.astype(bf16)
        new.append(jnp.dot(bgt, xw[:, g * gw:(g + 1) * gw], preferred_element_type=f32))
    h_ref[...] = h_ref[...] * decay_row + jnp.concatenate(new, axis=1)


def _state_readout(h_ref, bc, c_col0):
    gw = HEADS_PER_GROUP * SSD_HEAD_DIM
    outs = []
    for g in range(SSD_GROUPS):
        cg = bc[:, c_col0 + g * D_STATE:c_col0 + (g + 1) * D_STATE]
        outs.append(jnp.dot(cg, h_ref[:, g * gw:(g + 1) * gw].astype(bf16), preferred_element_type=f32))
    return jnp.concatenate(outs, axis=1)


C_COL0 = SSD_GROUPS * D_STATE


def _ssd_fwd_kernel(x_ref, bc_ref, dt_ref, bias_ref, arow_ref, dskip_ref, exf_ref, y_ref, h_ref):
    @pl.when(pl.program_id(1) == 0)
    def _():
        h_ref[...] = jnp.zeros_like(h_ref)

    x = x_ref[0]
    bc = bc_ref[0]
    dtv, cs = _ssd_cumsums(dt_ref, bias_ref, arow_ref)
    cst = cs.T
    dtt = dtv.T
    li = lax.broadcasted_iota(i32, (CHUNK, CHUNK), 0)
    si = lax.broadcasted_iota(i32, (CHUNK, CHUNK), 1)
    lower = li > si
    upper = li < si
    lower_eq = li >= si
    lane = lax.broadcasted_iota(i32, (CHUNK, LANES), 1)
    first_half = lane < SSD_HEAD_DIM

    cb = []
    for g in range(SSD_GROUPS):
        bg = bc[:, g * D_STATE:(g + 1) * D_STATE]
        cg = bc[:, C_COL0 + g * D_STATE:C_COL0 + (g + 1) * D_STATE]
        cb.append(lax.dot_general(cg, bg, (((1,), (1,)), ((), ())), preferred_element_type=f32))

    pairs = []
    for j in range(N_SSD_HEADS // 2):
        xpair = x[:, j * LANES:(j + 1) * LANES]
        acc = None
        for h, xm in ((2 * j, jnp.where(first_half, xpair, jnp.zeros_like(xpair))),
                      (2 * j + 1, jnp.where(first_half, jnp.zeros_like(xpair), xpair))):
            hb = N_SSD_HEADS + h
            arg = jnp.where(lower_eq, cs[:, h:h + 1] - cst[h:h + 1, :], cs[:, hb:hb + 1] - cst[hb:hb + 1, :])
            dtf = dtt[h:h + 1, :]
            dtb = dtt[hb:hb + 1, :]
            wgt = jnp.where(lower, dtf, jnp.where(upper, dtb, dtf + dtb))
            m = (cb[h // HEADS_PER_GROUP] * jnp.exp(arg) * wgt).astype(bf16)
            t = jnp.dot(m, xm, preferred_element_type=f32)
            acc = t if acc is None else acc + t
        pairs.append(acc)
    y = jnp.concatenate(pairs, axis=1)

    xf = x.astype(f32)
    e_f = _expand_heads(jnp.exp(cs), exf_ref)
    y = y + _state_readout(h_ref, bc, C_COL0) * e_f + dskip_ref[...] * xf
    y_ref[0] = y

    to_end = jnp.where(lane < N_SSD_HEADS, cs[CHUNK - 1:CHUNK, :] - cs, 0.0)
    w_f = _expand_heads(jnp.exp(to_end) * dtv, exf_ref)
    _state_update(h_ref, bc, (xf * w_f).astype(bf16), e_f[CHUNK - 1:CHUNK, :], 0)


def _ssd_bwd_kernel(x_ref, bc_ref, dt_ref, z_ref, yp_ref, bias_ref, arow_ref, exb_ref, nw_ref, o_ref, h_ref):
    @pl.when(pl.program_id(1) == 0)
    def _():
        h_ref[...] = jnp.zeros_like(h_ref)

    x = x_ref[0]
    bc = bc_ref[0]
    dtv, cs = _ssd_cumsums(dt_ref, bias_ref, arow_ref)
    e_b = _expand_heads(jnp.exp(cs), exb_ref)
    y = yp_ref[0] + _state_readout(h_ref, bc, C_COL0) * e_b

    lane = lax.broadcasted_iota(i32, (CHUNK, LANES), 1)
    to_start = jnp.where((lane >= N_SSD_HEADS) & (lane < 2 * N_SSD_HEADS), cs[0:1, :] - cs, 0.0)
    w_b = _expand_heads(jnp.exp(to_start) * dtv, exb_ref)
    _state_update(h_ref, bc, (x.astype(f32) * w_b).astype(bf16), e_b[0:1, :], 0)

    y = y * _silu(z_ref[0].astype(f32))
    gw = D_SSD // SSD_GROUPS
    outs = []
    for g in range(SSD_GROUPS):
        yg = y[:, g * gw:(g + 1) * gw]
        outs.append(yg * lax.rsqrt(jnp.mean(yg * yg, axis=-1, keepdims=True) + RMS_EPS))
    o_ref[0] = (jnp.concatenate(outs, axis=1) * nw_ref[...]).astype(bf16)


def _ssd(proj3, xconv, dt3, bias_row, a_row, dskip_row, ex_f, ex_b, norm_w_row):
    bsz, seq, _ = proj3.shape
    nc = seq // CHUNK
    bc_blk = D_SSD // (2 * SSD_GROUPS * D_STATE)
    row = lambda n: pl.BlockSpec((1, n), lambda b, c: (0, 0))
    ex_spec = pl.BlockSpec((2 * LANES, D_SSD), lambda b, c: (0, 0))
    fwd = lambda b, c: (b, c, 0)
    y_part = pl.pallas_call(
        _ssd_fwd_kernel,
        grid=(bsz, nc),
        in_specs=[
            pl.BlockSpec((1, CHUNK, D_SSD), fwd),
            pl.BlockSpec((1, CHUNK, 2 * SSD_GROUPS * D_STATE), lambda b, c: (b, c, bc_blk)),
            pl.BlockSpec((1, CHUNK, LANES), fwd),
            row(LANES), row(LANES), row(D_SSD), ex_spec,
        ],
        out_specs=pl.BlockSpec((1, CHUNK, D_SSD), fwd),
        out_shape=jax.ShapeDtypeStruct((bsz, seq, D_SSD), f32),
        scratch_shapes=[pltpu.VMEM((D_STATE, D_SSD), f32)],
        compiler_params=_params("parallel", "arbitrary"),
        name="ssd_forward_sweep",
    )(xconv, xconv, dt3, bias_row, a_row, dskip_row, ex_f)

    rev = lambda b, c: (b, nc - 1 - c, 0)
    return pl.pallas_call(
        _ssd_bwd_kernel,
        grid=(bsz, nc),
        in_specs=[
            pl.BlockSpec((1, CHUNK, D_SSD), rev),
            pl.BlockSpec((1, CHUNK, 2 * SSD_GROUPS * D_STATE), lambda b, c: (b, nc - 1 - c, bc_blk)),
            pl.BlockSpec((1, CHUNK, LANES), rev),
            pl.BlockSpec((1, CHUNK, D_SSD), rev),
            pl.BlockSpec((1, CHUNK, D_SSD), rev),
            row(LANES), row(LANES), ex_spec, row(D_SSD),
        ],
        out_specs=pl.BlockSpec((1, CHUNK, D_SSD), rev),
        out_shape=jax.ShapeDtypeStruct((bsz, seq, D_SSD), bf16),
        scratch_shapes=[pltpu.VMEM((D_STATE, D_SSD), f32)],
        compiler_params=_params("parallel", "arbitrary"),
        name="ssd_backward_sweep",
    )(xconv, xconv, dt3, proj3, y_part, bias_row, a_row, ex_b, norm_w_row)


ATTN_TQ = 256


def _attn_kernel(q_ref, k_ref, v_ref, slope_ref, lq1_ref, lk1_ref, lq2_ref, lk2_ref, nw_ref, o_ref, *, lambda_init):
    q = q_ref[0]
    k = k_ref[0]
    v = v_ref[0]
    tq = q.shape[0]
    seq = k.shape[0]
    lam = (jnp.exp(jnp.sum(lq1_ref[...] * lk1_ref[...], axis=-1, keepdims=True))
           - jnp.exp(jnp.sum(lq2_ref[...] * lk2_ref[...], axis=-1, keepdims=True)) + lambda_init)
    slope = slope_ref[0][:, 0:1]
    qpos = (pl.program_id(2) * tq + lax.broadcasted_iota(i32, (tq, seq), 0)).astype(f32)
    kpos = lax.broadcasted_iota(i32, (tq, seq), 1).astype(f32)
    bias = -slope * jnp.abs(qpos - kpos)
    lane = lax.broadcasted_iota(i32, q.shape, 1)
    first_half = lane < ATTN_HEAD_DIM
    zero = jnp.zeros_like(q)
    scale = ATTN_HEAD_DIM ** -0.5

    def softmax_map(qm):
        s = lax.dot_general(qm, k, (((1,), (1,)), ((), ())), preferred_element_type=f32) * scale + bias
        p = jnp.exp(s - jnp.max(s, axis=-1, keepdims=True))
        return p / jnp.sum(p, axis=-1, keepdims=True)

    att = softmax_map(jnp.where(first_half, q, zero)) - lam * softmax_map(jnp.where(first_half, zero, q))
    o = jnp.dot(att.astype(bf16), v, preferred_element_type=f32)
    o = o * lax.rsqrt(jnp.mean(o * o, axis=-1, keepdims=True) + RMS_EPS) * nw_ref[...] * (1.0 - lambda_init)
    o_ref[0] = o.astype(bf16)


def _diff_attention(proj3, slopes3, lq1, lk1, lq2, lk2, norm_w_row, lambda_init):
    bsz, seq, _ = proj3.shape
    tq = min(ATTN_TQ, seq)
    hd = 2 * ATTN_HEAD_DIM
    qb, kb, vb = COL_Q // hd, COL_K // hd, COL_V // hd
    vec = lambda n: pl.BlockSpec((1, n), lambda b, h, i: (0, 0))
    return pl.pallas_call(
        functools.partial(_attn_kernel, lambda_init=lambda_init),
        grid=(bsz, N_ATTN_HEADS, seq // tq),
        in_specs=[
            pl.BlockSpec((1, tq, hd), lambda b, h, i: (b, i, qb + h)),
            pl.BlockSpec((1, seq, hd), lambda b, h, i: (b, 0, kb + h)),
            pl.BlockSpec((1, seq, hd), lambda b, h, i: (b, 0, vb + h)),
            pl.BlockSpec((1, 1, LANES), lambda b, h, i: (h, 0, 0)),
            vec(ATTN_HEAD_DIM), vec(ATTN_HEAD_DIM), vec(ATTN_HEAD_DIM), vec(ATTN_HEAD_DIM), vec(hd),
        ],
        out_specs=pl.BlockSpec((1, tq, hd), lambda b, h, i: (b, i, h)),
        out_shape=jax.ShapeDtypeStruct((bsz, seq, D_ATTN), bf16),
        compiler_params=_params("parallel", "parallel", "arbitrary"),
        name="diff_attention",
    )(proj3, proj3, proj3, slopes3, lq1, lk1, lq2, lk2, norm_w_row)


def _outproj_kernel(ys_ref, ya_ref, x_ref, wa_ref, wb_ref, g_ref, b_ref, wr1_ref, wr2_ref, h_ref, lg_ref):
    mix = (jnp.dot(ys_ref[...], wa_ref[...], preferred_element_type=f32)
           + jnp.dot(ya_ref[...], wb_ref[...], preferred_element_type=f32))
    h = _layer_norm(ALPHA * x_ref[...] + mix, g_ref[...], b_ref[...])
    h_ref[...] = h
    h_hi = h.astype(bf16)
    h_lo = (h - h_hi.astype(f32)).astype(bf16)
    lg = (jnp.dot(h_hi, wr1_ref[...], preferred_element_type=f32)
          + jnp.dot(h_lo, wr2_ref[...], preferred_element_type=f32))
    lgt = lg.T
    lg_ref[...] = lgt[:N_EXPERTS] + lgt[N_EXPERTS:]


def _out_projection(y_ssd2, y_attn2, x2, w_a, w_b, g_row, b_row, w_router):
    n_tok = x2.shape[0]
    tm = min(512, n_tok)
    const = lambda shape: pl.BlockSpec(shape, lambda i: (0, 0), pipeline_mode=pl.Buffered(1))
    w_hi = w_router.astype(bf16)
    w_lo = (w_router - w_hi.astype(f32)).astype(bf16)
    wr1 = jnp.concatenate([w_hi, w_lo], axis=1)
    wr2 = jnp.concatenate([w_hi, jnp.zeros_like(w_hi)], axis=1)
    return pl.pallas_call(
        _outproj_kernel,
        grid=(n_tok // tm,),
        in_specs=[
            pl.BlockSpec((tm, D_SSD), lambda i: (i, 0)),
            pl.BlockSpec((tm, D_ATTN), lambda i: (i, 0)),
            pl.BlockSpec((tm, D_MODEL), lambda i: (i, 0)),
            const((D_SSD, D_MODEL)), const((D_ATTN, D_MODEL)),
            const((1, D_MODEL)), const((1, D_MODEL)),
            const((D_MODEL, 2 * N_EXPERTS)), const((D_MODEL, 2 * N_EXPERTS)),
        ],
        out_specs=[
            pl.BlockSpec((tm, D_MODEL), lambda i: (i, 0)),
            pl.BlockSpec((N_EXPERTS, tm), lambda i: (0, i)),
        ],
        out_shape=[
            jax.ShapeDtypeStruct((n_tok, D_MODEL), f32),
            jax.ShapeDtypeStruct((N_EXPERTS, n_tok), f32),
        ],
        compiler_params=_params("parallel"),
        name="out_projection_ln",
    )(y_ssd2, y_attn2, x2, w_a, w_b, g_row, b_row, wr1, wr2)


ROUTE_TT = 512


def _route_kernel(lg_ref, bias_ref, eidx_ref, gate_ref, rel_ref, cnt_ref, run_ref):
    @pl.when(pl.program_id(0) == 0)
    def _():
        run_ref[...] = jnp.zeros_like(run_ref)

    tt = lg_ref.shape[2]
    shape3 = (N_EXPERT_GROUPS, EXPERTS_PER_GROUP, tt)
    neg = -jnp.inf
    sc = _sigmoid(lg_ref[...])
    bi = sc + bias_ref[...]
    gidx = lax.broadcasted_iota(i32, shape3, 0).astype(f32)
    jidx = lax.broadcasted_iota(i32, shape3, 1).astype(f32)
    eid = gidx * EXPERTS_PER_GROUP + jidx

    m1 = jnp.max(bi, axis=1, keepdims=True)
    i1 = jnp.min(jnp.where(bi == m1, jidx, float(EXPERTS_PER_GROUP)), axis=1, keepdims=True)
    m2 = jnp.max(jnp.where(jidx == i1, neg, bi), axis=1, keepdims=True)
    gs = m1 + m2
    gcol = lax.broadcasted_iota(i32, gs.shape, 0).astype(f32)
    gmask = jnp.zeros(gs.shape, f32)
    for _ in range(TOPK_GROUPS):
        m = jnp.max(gs, axis=0, keepdims=True)
        ig = jnp.min(jnp.where(gs == m, gcol, float(N_EXPERT_GROUPS)), axis=0, keepdims=True)
        pick = gcol == ig
        gmask = jnp.where(pick, 1.0, gmask)
        gs = jnp.where(pick, neg, gs)

    val = jnp.where(gmask > 0.0, bi, neg)
    sel = jnp.zeros(shape3, f32)
    gates = []
    for k in range(TOP_K):
        m = jnp.max(jnp.max(val, axis=1, keepdims=True), axis=0, keepdims=True)
        ie = jnp.min(jnp.min(jnp.where(val == m, eid, float(N_EXPERTS)), axis=1, keepdims=True), axis=0, keepdims=True)
        pick = eid == ie
        gates.append(jnp.sum(jnp.sum(jnp.where(pick, sc, 0.0), axis=1, keepdims=True), axis=0, keepdims=True)[0])
        eidx_ref[k:k + 1, :] = ie[0].astype(i32)
        sel = jnp.where(pick, 1.0, sel)
        val = jnp.where(pick, neg, val)
    gsum = gates[0]
    for k in range(1, TOP_K):
        gsum = gsum + gates[k]
    for k in range(TOP_K):
        gate_ref[k:k + 1, :] = gates[k] / gsum * ROUTED_SCALE

    sel2 = sel.reshape(N_EXPERTS, tt)
    s_i = lax.broadcasted_iota(i32, (tt, tt), 0)
    t_i = lax.broadcasted_iota(i32, (tt, tt), 1)
    before = (s_i < t_i).astype(bf16)
    pos = jnp.dot(sel2.astype(bf16), before, preferred_element_type=f32) + run_ref[...]
    erow = lax.broadcasted_iota(i32, (N_EXPERTS, tt), 0)
    for k in range(TOP_K):
        pick2 = erow == eidx_ref[k:k + 1, :]
        rel_ref[k:k + 1, :] = jnp.sum(jnp.where(pick2, pos, 0.0), axis=0, keepdims=True).astype(i32)
    run_ref[...] = run_ref[...] + jnp.sum(sel2, axis=1, keepdims=True)
    cnt_ref[...] = jnp.broadcast_to(run_ref[...], cnt_ref.shape)


def _route(logits3, bias3):
    n_tok = logits3.shape[2]
    tt = min(ROUTE_TT, n_tok)
    tok_blk = pl.BlockSpec((TOP_K, tt), lambda i: (0, i))
    return pl.pallas_call(
        _route_kernel,
        grid=(n_tok // tt,),
        in_specs=[
            pl.BlockSpec((N_EXPERT_GROUPS, EXPERTS_PER_GROUP, tt), lambda i: (0, 0, i)),
            pl.BlockSpec((N_EXPERT_GROUPS, EXPERTS_PER_GROUP, 1), lambda i: (0, 0, 0)),
        ],
        out_specs=[tok_blk, tok_blk, tok_blk, pl.BlockSpec((N_EXPERTS, LANES), lambda i: (0, 0))],
        out_shape=[
            jax.ShapeDtypeStruct((TOP_K, n_tok), i32),
            jax.ShapeDtypeStruct((TOP_K, n_tok), f32),
            jax.ShapeDtypeStruct((TOP_K, n_tok), i32),
            jax.ShapeDtypeStruct((N_EXPERTS, LANES), f32),
        ],
        scratch_shapes=[pltpu.VMEM((N_EXPERTS, 1), f32)],
        compiler_params=_params("arbitrary"),
        name="route_topk",
    )(logits3, bias3)


def _dest_kernel(eidx_ref, rel_ref, pstart_ref, dest_ref):
    tt = eidx_ref.shape[1]
    erow = lax.broadcasted_iota(i32, (N_EXPERTS, tt), 0)
    pstart = pstart_ref[...]
    for k in range(TOP_K):
        pick = erow == eidx_ref[k:k + 1, :]
        base = jnp.sum(jnp.where(pick, pstart, 0.0), axis=0, keepdims=True)
        dest_ref[k:k + 1, :] = rel_ref[k:k + 1, :] + base.astype(i32)


def _dest_rows(eidx, rel, pstart_col):
    n_tok = eidx.shape[1]
    tt = min(ROUTE_TT, n_tok)
    tok_blk = pl.BlockSpec((TOP_K, tt), lambda i: (0, i))
    return pl.pallas_call(
        _dest_kernel,
        grid=(n_tok // tt,),
        in_specs=[tok_blk, tok_blk, pl.BlockSpec((N_EXPERTS, 1), lambda i: (0, 0))],
        out_specs=tok_blk,
        out_shape=jax.ShapeDtypeStruct((TOP_K, n_tok), i32),
        compiler_params=_params("parallel"),
        name="dest_rows",
    )(eidx, rel, pstart_col)


MOE_TM = 256
DISPATCH_TM = 512


def _tile_slots(dest, tm):
    return dest.reshape(TOP_K, dest.shape[1] // tm, tm).transpose(1, 0, 2).reshape(-1)


def _dispatch_kernel(dest_ref, h_ref, xs_ref, sem):
    tm = h_ref.shape[0]

    def issue(t, carry):
        for k in range(TOP_K):
            pltpu.make_async_copy(h_ref.at[pl.ds(t, 1)], xs_ref.at[pl.ds(dest_ref[k * tm + t], 1)], sem).start()
        return carry

    lax.fori_loop(0, tm, issue, 0)
    for _ in range(TOP_K):
        pltpu.make_async_copy(h_ref, xs_ref.at[pl.ds(0, tm)], sem).wait()


def _dispatch(dest, h1, n_rows):
    n_tok = h1.shape[0]
    tm = min(DISPATCH_TM, n_tok)
    return pl.pallas_call(
        _dispatch_kernel,
        grid=(n_tok // tm,),
        in_specs=[
            pl.BlockSpec((TOP_K * tm,), lambda i: (i,), memory_space=pltpu.SMEM),
            pl.BlockSpec((tm, D_MODEL), lambda i: (i, 0)),
        ],
        out_specs=pl.BlockSpec(memory_space=pl.ANY),
        out_shape=jax.ShapeDtypeStruct((n_rows, D_MODEL), f32),
        scratch_shapes=[pltpu.SemaphoreType.DMA(())],
        compiler_params=_params("arbitrary"),
        name="moe_dispatch",
    )(_tile_slots(dest, tm), h1)


def _expert_kernel(bexp_ref, nused_ref, next_ref, x_ref, wg_hbm, wu_hbm, wd_hbm, o_ref,
                   stage_g, stage_u, stage_d, wg_s, wu_s, wd_s, sem):
    i = pl.program_id(0)
    e = bexp_ref[i]
    first_block_of_expert = (i == 0) | (e != bexp_ref[jnp.maximum(i - 1, 0)])

    def weight_copies(expert):
        return (pltpu.make_async_copy(wg_hbm.at[expert], stage_g, sem.at[0]),
                pltpu.make_async_copy(wu_hbm.at[expert], stage_u, sem.at[1]),
                pltpu.make_async_copy(wd_hbm.at[expert], stage_d, sem.at[2]))

    @pl.when(i == 0)
    def _():
        for c in weight_copies(e):
            c.start()

    @pl.when(first_block_of_expert)
    def _():
        for c in weight_copies(e):
            c.wait()
        wg_s[...] = stage_g[...].astype(bf16)
        wu_s[...] = stage_u[...].astype(bf16)
        wd_s[...] = stage_d[...].astype(bf16)

        @pl.when(next_ref[i] >= 0)
        def _():
            for c in weight_copies(next_ref[i]):
                c.start()

    @pl.when(i < nused_ref[0])
    def _():
        xb = x_ref[...].astype(bf16)
        hg = jnp.dot(xb, wg_s[...], preferred_element_type=f32)
        hu = jnp.dot(xb, wu_s[...], preferred_element_type=f32)
        hb = (_silu(hg) * hu).astype(bf16)
        o_ref[...] = jnp.dot(hb, wd_s[...], preferred_element_type=f32)


def _expert_ffn(block_exp, n_used, next_exp, x_sorted, w_gate_e, w_up_e, w_down_e):
    n_blocks = x_sorted.shape[0] // BLOCK_ROWS
    rows = lambda i, be, nu, nx: (jnp.minimum(i, nu[0] - 1), 0)
    blk = pl.BlockSpec((BLOCK_ROWS, D_MODEL), rows)
    hbm = pl.BlockSpec(memory_space=pl.ANY)
    return pl.pallas_call(
        _expert_kernel,
        grid_spec=pltpu.PrefetchScalarGridSpec(
            num_scalar_prefetch=3,
            grid=(n_blocks,),
            in_specs=[blk, hbm, hbm, hbm],
            out_specs=blk,
            scratch_shapes=[
                pltpu.VMEM((D_MODEL, D_EXPERT), f32),
                pltpu.VMEM((D_MODEL, D_EXPERT), f32),
                pltpu.VMEM((D_EXPERT, D_MODEL), f32),
                pltpu.VMEM((D_MODEL, D_EXPERT), bf16),
                pltpu.VMEM((D_MODEL, D_EXPERT), bf16),
                pltpu.VMEM((D_EXPERT, D_MODEL), bf16),
                pltpu.SemaphoreType.DMA((3,)),
            ],
        ),
        out_shape=jax.ShapeDtypeStruct(x_sorted.shape, f32),
        compiler_params=_params("arbitrary"),
        name="moe_expert_ffn",
    )(block_exp, n_used, next_exp, x_sorted, w_gate_e, w_up_e, w_down_e)


def _combine_kernel(dest_ref, next_ref, gate_ref, h_ref, wg_ref, wu_ref, wd_ref, g_ref, b_ref, ys_ref, o_ref, buf, sem):
    tm = h_ref.shape[0]
    i = pl.program_id(0)
    slot = i % 2

    def gather(slots_ref, dst_slot):
        def issue(t, carry):
            for k in range(TOP_K):
                pltpu.make_async_copy(ys_ref.at[pl.ds(slots_ref[k * tm + t], 1)], buf.at[dst_slot, k, pl.ds(t, 1)],
                                      sem.at[dst_slot]).start()
            return carry

        lax.fori_loop(0, tm, issue, 0)

    @pl.when(i == 0)
    def _():
        gather(dest_ref, 0)

    @pl.when(i + 1 < pl.num_programs(0))
    def _():
        gather(next_ref, 1 - slot)

    h = h_ref[...]
    hb = h.astype(bf16)
    hg = jnp.dot(hb, wg_ref[...], preferred_element_type=f32)
    hu = jnp.dot(hb, wu_ref[...], preferred_element_type=f32)
    ffn = jnp.dot((_silu(hg) * hu).astype(bf16), wd_ref[...], preferred_element_type=f32)

    for k in range(TOP_K):
        pltpu.make_async_copy(ys_ref.at[pl.ds(0, tm)], buf.at[slot, k], sem.at[slot]).wait()
    gate = gate_ref[...]
    for k in range(TOP_K):
        ffn = ffn + gate[:, k:k + 1] * buf[slot, k]
    o_ref[...] = _layer_norm(ALPHA * h + ffn, g_ref[...], b_ref[...])


def _combine(dest, gate_t, h1, w_gate_s, w_up_s, w_down_s, g_row, b_row, y_sorted):
    n_tok = h1.shape[0]
    tm = min(MOE_TM, n_tok)
    const = lambda shape: pl.BlockSpec(shape, lambda i: (0, 0), pipeline_mode=pl.Buffered(1))
    n_tiles = n_tok // tm
    slots = _tile_slots(dest, tm)
    return pl.pallas_call(
        _combine_kernel,
        grid=(n_tiles,),
        in_specs=[
            pl.BlockSpec((TOP_K * tm,), lambda i: (i,), memory_space=pltpu.SMEM),
            pl.BlockSpec((TOP_K * tm,), lambda i: (jnp.minimum(i + 1, n_tiles - 1),), memory_space=pltpu.SMEM),
            pl.BlockSpec((tm, TOP_K), lambda i: (i, 0)),
            pl.BlockSpec((tm, D_MODEL), lambda i: (i, 0)),
            const((D_MODEL, D_EXPERT)), const((D_MODEL, D_EXPERT)), const((D_EXPERT, D_MODEL)),
            const((1, D_MODEL)), const((1, D_MODEL)),
            pl.BlockSpec(memory_space=pl.ANY),
        ],
        out_specs=pl.BlockSpec((tm, D_MODEL), lambda i: (i, 0)),
        out_shape=jax.ShapeDtypeStruct((n_tok, D_MODEL), f32),
        scratch_shapes=[pltpu.VMEM((2, TOP_K, tm, D_MODEL), f32), pltpu.SemaphoreType.DMA((2,))],
        compiler_params=_params("arbitrary"),
        name="moe_combine_ln",
    )(slots, slots, gate_t, h1, w_gate_s, w_up_s, w_down_s, g_row, b_row, y_sorted)


def _head_expansion(first_lane):
    r = jnp.arange(LANES)[:, None]
    c = jnp.arange(D_SSD)[None, :] // SSD_HEAD_DIM
    ex = (r == c + first_lane).astype(bf16)
    return jnp.concatenate([ex, ex], axis=0)


def _pad_lanes(v):
    return jnp.pad(v.astype(f32), (0, LANES - v.shape[0]))[None, :]


def _layer(h3, w_in, conv_w, conv_b, dt_bias_f, dt_bias_b, a_log_f, a_log_b, d_skip, ssd_norm_w,
           lambda_q1, lambda_k1, lambda_q2, lambda_k2, attn_norm_w, w_out, ln1_g, ln1_b,
           w_router, router_bias, w_gate_e, w_up_e, w_down_e, w_gate_s, w_up_s, w_down_s,
           ln2_g, ln2_b, lambda_init):
    bsz, seq, d = h3.shape
    n_tok = bsz * seq
    x2 = h3.reshape(n_tok, d)

    n_dt = 2 * N_SSD_HEADS
    c_dt = D_SSD + CONV_CH
    w_main = jnp.concatenate([w_in[:, :c_dt], w_in[:, c_dt + n_dt:]], axis=1).astype(bf16)
    w_dt = jnp.pad(w_in[:, c_dt:c_dt + n_dt], ((0, 0), (0, LANES - n_dt))).astype(bf16)
    proj, dt = _in_projection(x2, w_main, w_dt)
    proj3 = proj.reshape(bsz, seq, PROJ_MAIN)
    dt3 = dt.reshape(bsz, seq, LANES)

    conv_w8 = jnp.pad(conv_w.astype(f32), ((0, 8 - CONV_W), (0, 0)))
    xconv = _conv_silu(proj3, conv_w8, conv_b.astype(f32)[None, :])
    bias_row = _pad_lanes(jnp.concatenate([dt_bias_f, dt_bias_b]))
    a_row = _pad_lanes(jnp.concatenate([-jnp.exp(a_log_f.astype(f32)), -jnp.exp(a_log_b.astype(f32))]))
    dskip_row = jnp.repeat(d_skip.astype(f32), SSD_HEAD_DIM)[None, :]
    y_ssd = _ssd(proj3, xconv, dt3, bias_row, a_row, dskip_row, _head_expansion(0), _head_expansion(N_SSD_HEADS),
                 ssd_norm_w.astype(f32)[None, :])

    slopes = 2.0 ** (-8.0 * jnp.arange(1, N_ATTN_HEADS + 1, dtype=f32) / N_ATTN_HEADS)
    slopes3 = jnp.broadcast_to(slopes[:, None, None], (N_ATTN_HEADS, 1, LANES))
    vec = lambda v: v.astype(f32)[None, :]
    y_attn = _diff_attention(proj3, slopes3, vec(lambda_q1), vec(lambda_k1), vec(lambda_q2), vec(lambda_k2),
                             vec(attn_norm_w), lambda_init)

    w_out_b = w_out.astype(bf16)
    h1, logits_t = _out_projection(
        y_ssd.reshape(n_tok, D_SSD), y_attn.reshape(n_tok, D_ATTN), x2, w_out_b[:D_SSD], w_out_b[D_SSD:],
        vec(ln1_g), vec(ln1_b), w_router.astype(f32))

    eidx, gate, rel, counts = _route(
        logits_t.reshape(N_EXPERT_GROUPS, EXPERTS_PER_GROUP, n_tok),
        router_bias.astype(f32).reshape(N_EXPERT_GROUPS, EXPERTS_PER_GROUP, 1))
    counts = counts[:, 0].astype(i32)
    padded = (counts + BLOCK_ROWS - 1) // BLOCK_ROWS * BLOCK_ROWS
    pends = jnp.cumsum(padded)
    pstarts = pends - padded
    n_blocks = -(-(n_tok * TOP_K) // BLOCK_ROWS) + N_EXPERTS
    n_used = (pends[-1] // BLOCK_ROWS).astype(i32)
    blk = jnp.minimum(jnp.arange(n_blocks, dtype=i32), n_used - 1)
    block_exp = jnp.minimum(jnp.sum((pends[None, :] <= (blk * BLOCK_ROWS)[:, None]).astype(i32), axis=1),
                            N_EXPERTS - 1)
    seg_end = pends[block_exp] // BLOCK_ROWS
    next_exp = jnp.where(seg_end < n_used, block_exp[jnp.minimum(seg_end, n_blocks - 1)], -1).astype(i32)
    dest = _dest_rows(eidx, rel, pstarts[:, None].astype(f32))

    x_sorted = _dispatch(dest, h1, n_blocks * BLOCK_ROWS)
    y_sorted = _expert_ffn(block_exp, n_used[None], next_exp, x_sorted, w_gate_e, w_up_e, w_down_e)
    out = _combine(dest, gate.T, h1, w_gate_s.astype(bf16), w_up_s.astype(bf16), w_down_s.astype(bf16),
                   vec(ln2_g), vec(ln2_b), y_sorted)
    return out.reshape(bsz, seq, d)


def kernel(x, w_in, conv_w, conv_b, dt_bias_f, dt_bias_b, a_log_f, a_log_b, d_skip, ssd_norm_w, lambda_q1, lambda_k1, lambda_q2, lambda_k2, attn_norm_w, w_out, ln1_g, ln1_b, w_router, router_bias, w_gate_e, w_up_e, w_down_e, w_gate_s, w_up_s, w_down_s, ln2_g, ln2_b):
    h = x
    for l in range(DEPTH):
        lambda_init = 0.8 - 0.6 * math.exp(-0.3 * l)
        h = _layer(h, w_in[l], conv_w[l], conv_b[l], dt_bias_f[l], dt_bias_b[l], a_log_f[l], a_log_b[l],
                   d_skip[l], ssd_norm_w[l], lambda_q1[l], lambda_k1[l], lambda_q2[l], lambda_k2[l],
                   attn_norm_w[l], w_out[l], ln1_g[l], ln1_b[l], w_router[l], router_bias[l],
                   w_gate_e[l], w_up_e[l], w_down_e[l], w_gate_s[l], w_up_s[l], w_down_s[l],
                   ln2_g[l], ln2_b[l], lambda_init)
    return h
```
